```python
import jax, jax.numpy as jnp
from jax import lax
import numpy as np

D_MODEL = 2048
BATCH = 4
SEQ = 4096
DEPTH = 1

LRU_WIDTH = D_MODEL // 2
LRU_BLOCKS = 8
LRU_BLOCK = LRU_WIDTH // LRU_BLOCKS
CONV_WIDTH = 4
LRU_C = 8.0
ATTN_HEAD_DIM = 64
ATTN_Q_HEADS = (D_MODEL - LRU_WIDTH) // ATTN_HEAD_DIM
ATTN_KV_HEADS = 2
ATTN_GROUP = ATTN_Q_HEADS // ATTN_KV_HEADS
WINDOW = 128
ATTN_BLOCK = 128
ROPE_THETA = 500000.0
ROPE_DIM = ATTN_HEAD_DIM // 4
Q_WIDTH = ATTN_Q_HEADS * ATTN_HEAD_DIM
KV_WIDTH = ATTN_KV_HEADS * ATTN_HEAD_DIM
MIX_IN = 2 * LRU_WIDTH + Q_WIDTH + 2 * KV_WIDTH
MIX_WIDTH = LRU_WIDTH + Q_WIDTH
PEER_HEADS = 8
PEER_N_KEYS = 128
PEER_N_EXPERTS = PEER_N_KEYS * PEER_N_KEYS
PEER_QUERY_DIM = 256
PEER_HALF = PEER_QUERY_DIM // 2
PEER_TOPK = 16
PEER_TOKEN_CHUNK = 128
NORM_EPS = 1e-6

kernel_name = 'hymba_rglru_swa_sink_peer_adaln'


def rms_norm(x, g):
    xf = x.astype(jnp.float32)
    y = xf * lax.rsqrt(jnp.mean(xf * xf, axis=-1, keepdims=True) + NORM_EPS)
    return (y * g.astype(jnp.float32)).astype(x.dtype)


def modulate(xn, shift, scale):
    return xn * (1.0 + scale[:, None, :]) + shift[:, None, :]


def partial_rope(t, positions):
    half = ROPE_DIM // 2
    inv_freq = jnp.power(ROPE_THETA, -2.0 * jnp.arange(half, dtype=jnp.float32) / ROPE_DIM)
    ang = positions.astype(jnp.float32)[..., None] * inv_freq
    cos = jnp.cos(ang)[:, :, None, :]
    sin = jnp.sin(ang)[:, :, None, :]
    rot = t[..., :ROPE_DIM].astype(jnp.float32)
    x1, x2 = rot[..., :half], rot[..., half:]
    rotated = jnp.concatenate([x1 * cos - x2 * sin, x2 * cos + x1 * sin], axis=-1)
    return jnp.concatenate([rotated.astype(t.dtype), t[..., ROPE_DIM:]], axis=-1)


def rg_lru_group(x_in, conv_w, conv_b, gate_a_w, gate_a_b, gate_x_w, gate_x_b, lam):
    B, S, _ = x_in.shape
    xp = jnp.pad(x_in, ((0, 0), (CONV_WIDTH - 1, 0), (0, 0)))
    xc = conv_b
    for k in range(CONV_WIDTH):
        xc = xc + conv_w[k] * xp[:, k:k + S]
    xb = xc.reshape(B, S, LRU_BLOCKS, LRU_BLOCK)
    r = jax.nn.sigmoid(jnp.einsum('bshi,hij->bshj', xb, gate_a_w).reshape(B, S, LRU_WIDTH) + gate_a_b)
    i = jax.nn.sigmoid(jnp.einsum('bshi,hij->bshj', xb, gate_x_w).reshape(B, S, LRU_WIDTH) + gate_x_b)
    log_a = -LRU_C * r.astype(jnp.float32) * jax.nn.softplus(-lam.astype(jnp.float32))
    a = jnp.exp(log_a)
    b = jnp.sqrt(-jnp.expm1(2.0 * log_a)) * (i * xc).astype(jnp.float32)

    def combine(left, right):
        a_l, b_l = left
        a_r, b_r = right
        return a_l * a_r, a_r * b_l + b_r

    _, h = lax.associative_scan(combine, (a, b), axis=1)
    return h.astype(x_in.dtype)


def sliding_window_attention(q, k, v, sinks):
    B, S, HQ, HD = q.shape
    nblk = S // ATTN_BLOCK
    qb = q.reshape(B, nblk, ATTN_BLOCK, ATTN_KV_HEADS, ATTN_GROUP, HD)

    def band(t):
        tp = jnp.pad(t, ((0, 0), (ATTN_BLOCK, 0), (0, 0), (0, 0)))
        tb = tp.reshape(B, nblk + 1, ATTN_BLOCK, ATTN_KV_HEADS, HD)
        return jnp.concatenate([tb[:, :-1], tb[:, 1:]], axis=2)

    kb, vb = band(k), band(v)
    scores = jnp.einsum('bnqhgd,bnkhd->bnhgqk', qb, kb).astype(jnp.float32) * (HD ** -0.5)
    q_i = jnp.arange(ATTN_BLOCK)[:, None]
    k_j = jnp.arange(2 * ATTN_BLOCK)[None, :]
    rel = q_i + ATTN_BLOCK - k_j
    key_pos = jnp.arange(nblk)[:, None, None] * ATTN_BLOCK + k_j[None] - ATTN_BLOCK
    valid = (rel >= 0) & (rel < WINDOW) & (key_pos >= 0)
    scores = jnp.where(valid[None, :, None, None], scores, -jnp.inf)
    sink = jnp.broadcast_to(sinks.astype(jnp.float32).reshape(1, 1, ATTN_KV_HEADS, ATTN_GROUP, 1, 1),
                            scores.shape[:-1] + (1,))
    probs = jax.nn.softmax(jnp.concatenate([scores, sink], axis=-1), axis=-1)[..., :-1]
    out = jnp.einsum('bnhgqk,bnkhd->bnqhgd', probs.astype(v.dtype), vb)
    return out.reshape(B, S, HQ * HD)


def peer_ffn(t, w_q, sub_keys, u_tab, v_tab):
    B, S, D = t.shape
    T = B * S
    tf = t.reshape(T, D)
    q = (tf @ w_q).reshape(T, PEER_HEADS, 2, PEER_HALF).astype(jnp.float32)
    s = jnp.einsum('thpd,hpnd->thpn', q, sub_keys.astype(jnp.float32))
    sv, si = lax.top_k(s, PEER_TOPK)
    cand_s = (sv[:, :, 0, :, None] + sv[:, :, 1, None, :]).reshape(T, PEER_HEADS, PEER_TOPK * PEER_TOPK)
    cand_i = (si[:, :, 0, :, None] * PEER_N_KEYS + si[:, :, 1, None, :]).reshape(T, PEER_HEADS, PEER_TOPK * PEER_TOPK)
    top_s, pos = lax.top_k(cand_s, PEER_TOPK)
    idx = jnp.take_along_axis(cand_i, pos, axis=-1)
    g = jax.nn.softmax(top_s, axis=-1)
    nch = T // PEER_TOKEN_CHUNK

    def expert_chunk(args):
        xc, ic, gc = args
        u = u_tab[ic]
        act = jax.nn.gelu(jnp.einsum('cd,chkd->chk', xc, u).astype(jnp.float32))
        w = (gc * act).astype(xc.dtype)
        return jnp.einsum('chk,chkd->cd', w, v_tab[ic])

    out = lax.map(expert_chunk, (tf.reshape(nch, PEER_TOKEN_CHUNK, D),
                                 idx.reshape(nch, PEER_TOKEN_CHUNK, PEER_HEADS, PEER_TOPK),
                                 g.reshape(nch, PEER_TOKEN_CHUNK, PEER_HEADS, PEER_TOPK)))
    return out.reshape(B, S, D)


def hybrid_layer(x, c, positions, w_ada, b_ada, norm1_g, w_in, conv_w, conv_b,
                 gate_a_w, gate_a_b, gate_x_w, gate_x_b, lam, q_norm_g, k_norm_g, sinks,
                 out_norm_lru_g, out_norm_attn_g, w_out, norm2_g, peer_w_q, peer_sub_keys,
                 peer_u, peer_v):
    B, S, _ = x.shape
    ada = jax.nn.silu(c) @ w_ada + b_ada
    sh1, sc1, g1, sh2, sc2, g2 = jnp.split(ada, 6, axis=-1)

    h = modulate(rms_norm(x, norm1_g), sh1, sc1)
    proj = h @ w_in
    splits = np.cumsum([LRU_WIDTH, LRU_WIDTH, Q_WIDTH, KV_WIDTH]).tolist()
    x_lru, y_lru, q, k, v = jnp.split(proj, splits, axis=-1)
    lru = rg_lru_group(x_lru, conv_w, conv_b, gate_a_w, gate_a_b, gate_x_w, gate_x_b, lam)
    lru = lru * jax.nn.gelu(y_lru)
    q = q.reshape(B, S, ATTN_Q_HEADS, ATTN_HEAD_DIM)
    k = k.reshape(B, S, ATTN_KV_HEADS, ATTN_HEAD_DIM)
    v = v.reshape(B, S, ATTN_KV_HEADS, ATTN_HEAD_DIM)
    q = partial_rope(rms_norm(q, q_norm_g), positions)
    k = partial_rope(rms_norm(k, k_norm_g), positions)
    attn = sliding_window_attention(q, k, v, sinks)
    mixed = jnp.concatenate([rms_norm(lru, out_norm_lru_g), rms_norm(attn, out_norm_attn_g)], axis=-1)
    x = x + g1[:, None, :] * (mixed @ w_out)

    h2 = modulate(rms_norm(x, norm2_g), sh2, sc2)
    x = x + g2[:, None, :] * peer_ffn(h2, peer_w_q, peer_sub_keys, peer_u, peer_v)
    return x


def setup_inputs(seed: int = 0) -> dict:
    key = jax.random.key(seed)
    ks = jax.random.split(key, 32)
    f32 = jnp.float32
    L, D = DEPTH, D_MODEL

    def nrm(k, shape, scale):
        return jax.random.normal(k, shape, f32) * scale

    def gain(k, shape):
        return 1.0 + 0.02 * jax.random.normal(k, shape, f32)

    a_c = jax.random.uniform(ks[10], (L, LRU_WIDTH), f32, minval=0.9, maxval=0.999)
    a0 = a_c ** (1.0 / LRU_C)
    lru_lambda = jnp.log(a0) - jnp.log1p(-a0)
    offsets = jax.random.randint(ks[3], (BATCH, 1), 0, 1024, dtype=jnp.int32)
    positions = offsets + jnp.arange(SEQ, dtype=jnp.int32)[None, :]
    return {
        'x': nrm(ks[0], (BATCH, SEQ, D), 1.0),
        'c': nrm(ks[1], (BATCH, D), 1.0),
        'positions': positions,
        'w_ada': nrm(ks[2], (L, D, 6 * D), 0.5 * D ** -0.5),
        'b_ada': nrm(ks[4], (L, 6 * D), 0.02),
        'norm1_g': gain(ks[5], (L, D)),
        'w_in': nrm(ks[6], (L, D, MIX_IN), D ** -0.5),
        'conv_w': nrm(ks[7], (L, CONV_WIDTH, LRU_WIDTH), CONV_WIDTH ** -0.5),
        'conv_b': nrm(ks[8], (L, LRU_WIDTH), 0.02),
        'lru_gate_a_w': nrm(ks[9], (L, LRU_BLOCKS, LRU_BLOCK, LRU_BLOCK), LRU_BLOCK ** -0.5),
        'lru_gate_a_b': nrm(ks[11], (L, LRU_WIDTH), 0.02),
        'lru_gate_x_w': nrm(ks[12], (L, LRU_BLOCKS, LRU_BLOCK, LRU_BLOCK), LRU_BLOCK ** -0.5),
        'lru_gate_x_b': nrm(ks[13], (L, LRU_WIDTH), 0.02),
        'lru_lambda': lru_lambda,
        'q_norm_g': gain(ks[14], (L, ATTN_HEAD_DIM)),
        'k_norm_g': gain(ks[15], (L, ATTN_HEAD_DIM)),
        'attn_sinks': nrm(ks[16], (L, ATTN_Q_HEADS), 0.5),
        'out_norm_lru_g': gain(ks[17], (L, LRU_WIDTH)),
        'out_norm_attn_g': gain(ks[18], (L, Q_WIDTH)),
        'w_out': nrm(ks[19], (L, MIX_WIDTH, D), MIX_WIDTH ** -0.5),
        'norm2_g': gain(ks[20], (L, D)),
        'peer_w_q': nrm(ks[21], (L, D, PEER_HEADS * PEER_QUERY_DIM), D ** -0.5),
        'peer_sub_keys': nrm(ks[22], (L, PEER_HEADS, 2, PEER_N_KEYS, PEER_HALF), PEER_HALF ** -0.5),
        'peer_u': nrm(ks[23], (L, PEER_N_EXPERTS, D), D ** -0.5),
        'peer_v': nrm(ks[24], (L, PEER_N_EXPERTS, D), PEER_HEADS ** -0.5),
    }


def reference(x, c, positions, w_ada, b_ada, norm1_g, w_in, conv_w, conv_b,
              lru_gate_a_w, lru_gate_a_b, lru_gate_x_w, lru_gate_x_b, lru_lambda,
              q_norm_g, k_norm_g, attn_sinks, out_norm_lru_g, out_norm_attn_g, w_out,
              norm2_g, peer_w_q, peer_sub_keys, peer_u, peer_v):
    for l in range(DEPTH):
        x = hybrid_layer(x, c, positions, w_ada[l], b_ada[l], norm1_g[l], w_in[l], conv_w[l], conv_b[l],
                         lru_gate_a_w[l], lru_gate_a_b[l], lru_gate_x_w[l], lru_gate_x_b[l], lru_lambda[l],
                         q_norm_g[l], k_norm_g[l], attn_sinks[l], out_norm_lru_g[l], out_norm_attn_g[l],
                         w_out[l], norm2_g[l], peer_w_q[l], peer_sub_keys[l], peer_u[l], peer_v[l])
    return x
```

```python
import functools
import math

import jax
import jax.numpy as jnp
from jax import lax
from jax.experimental import pallas as pl
from jax.experimental.pallas import tpu as pltpu

F32 = jnp.float32
BF16 = jnp.bfloat16

LRU_BLOCKS = 8
CONV_WIDTH = 4
LRU_C = 8.0
HEAD_DIM = 64
KV_HEADS = 2
WINDOW = 128
ROPE_THETA = 500000.0
ROPE_DIM = HEAD_DIM // 4
PEER_HEADS = 8
PEER_KEYS = 128
PEER_TOPK = 16
NORM_EPS = 1e-6

LANES = 128
SUBLANES = 8
VMEM_LIMIT = 56 * 1024 * 1024

NEG_BIG = -1e30


def _dot(a, b):
    return jnp.dot(a, b, preferred_element_type=F32)


def _dot_nt(a, b):
    return lax.dot_general(a, b, (((1,), (1,)), ((), ())), preferred_element_type=F32)


def _split(x):
    hi = x.astype(BF16)
    lo = (x - hi.astype(F32)).astype(BF16)
    return hi, lo


def _dot3(a, b):
    ah, al = _split(a)
    bh, bl = _split(b)
    return _dot(ah, bh) + (_dot(ah, bl) + _dot(al, bh))


def _sigmoid(x):
    return 1.0 / (1.0 + jnp.exp(-x))


def _gelu_tanh(x):
    c = math.sqrt(2.0 / math.pi)
    return 0.5 * x * (1.0 + jnp.tanh(c * (x + 0.044715 * (x * x * x))))


def _params(*sem):
    return pltpu.CompilerParams(dimension_semantics=sem, vmem_limit_bytes=VMEM_LIMIT)


def _ada_kernel(c_ref, w_ref, b_ref, o_ref):
    c = c_ref[...]
    s = c * _sigmoid(c)
    o_ref[...] = _dot3(s, w_ref[...]) + b_ref[...]


def _ada(c_pad, w_ada, b_ada, tn=1024):
    m, d = c_pad.shape
    n = w_ada.shape[1]
    return pl.pallas_call(
        _ada_kernel,
        grid=(n // tn,),
        in_specs=[pl.BlockSpec((m, d), lambda j: (0, 0)),
                  pl.BlockSpec((d, tn), lambda j: (0, j)),
                  pl.BlockSpec((1, tn), lambda j: (0, j))],
        out_specs=pl.BlockSpec((m, tn), lambda j: (0, j)),
        out_shape=jax.ShapeDtypeStruct((m, n), F32),
        compiler_params=_params("arbitrary"),
        name="ada",
    )(c_pad, w_ada, b_ada)


def _inproj_kernel(x_ref, g_ref, sh_ref, sc_ref, w_ref, xl_ref, yl_ref, q_ref, kv_ref, *, lw, qw):
    x = x_ref[...]
    ms = jnp.mean(x * x, axis=-1, keepdims=True)
    xn = x * lax.rsqrt(ms + NORM_EPS) * g_ref[...]
    h = (xn * (1.0 + sc_ref[0]) + sh_ref[0]).astype(BF16)
    xl_ref[...] = _dot(h, w_ref[:, 0:lw])
    yl_ref[...] = _dot(h, w_ref[:, lw:2 * lw])
    q_ref[...] = _dot(h, w_ref[:, 2 * lw:2 * lw + qw])
    kv_ref[...] = _dot(h, w_ref[:, 2 * lw + qw:])


def _inproj(x2, norm_g, sh, sc, w_in, seq, lw, qw, tm=256):
    t, d = x2.shape
    n = w_in.shape[1]
    kvw = n - 2 * lw - qw
    bs = seq // tm
    row = lambda i: (i, 0)
    vec = lambda i: (i // bs, 0, 0)
    return pl.pallas_call(
        functools.partial(_inproj_kernel, lw=lw, qw=qw),
        grid=(t // tm,),
        in_specs=[pl.BlockSpec((tm, d), row),
                  pl.BlockSpec((1, d), lambda i: (0, 0)),
                  pl.BlockSpec((1, 1, d), vec),
                  pl.BlockSpec((1, 1, d), vec),
                  pl.BlockSpec((d, n), lambda i: (0, 0))],
        out_specs=[pl.BlockSpec((tm, lw), row), pl.BlockSpec((tm, lw), row),
                   pl.BlockSpec((tm, qw), row), pl.BlockSpec((tm, kvw), row)],
        out_shape=[jax.ShapeDtypeStruct((t, lw), F32), jax.ShapeDtypeStruct((t, lw), F32),
                   jax.ShapeDtypeStruct((t, qw), F32), jax.ShapeDtypeStruct((t, kvw), F32)],
        compiler_params=_params("arbitrary"),
        name="inproj",
    )(x2, norm_g, sh, sc, w_in)


def _lru_kernel(xl_ref, yl_ref, cw_ref, cb_ref, wa_ref, ba_ref, wx_ref, bx_ref, lam_ref, g_ref,
                o_ref, xext, hcar, *, tc):
    lw = xl_ref.shape[1]
    blk = lw // LRU_BLOCKS

    @pl.when(pl.program_id(1) == 0)
    def _():
        xext[0:SUBLANES, :] = jnp.zeros((SUBLANES, lw), F32)
        hcar[...] = jnp.zeros_like(hcar)

    x = xl_ref[...]
    xext[SUBLANES:SUBLANES + tc, :] = x
    xc = cb_ref[...] + cw_ref[CONV_WIDTH - 1:CONV_WIDTH, :] * x
    for k in range(CONV_WIDTH - 1):
        off = SUBLANES - (CONV_WIDTH - 1) + k
        xc = xc + cw_ref[k:k + 1, :] * xext[off:off + tc, :]
    xext[0:SUBLANES, :] = x[tc - SUBLANES:tc, :]

    ra, rx = [], []
    for hb in range(LRU_BLOCKS):
        xb = xc[:, hb * blk:(hb + 1) * blk].astype(BF16)
        ra.append(_dot(xb, wa_ref[hb]))
        rx.append(_dot(xb, wx_ref[hb]))
    r = _sigmoid(jnp.concatenate(ra, axis=1) + ba_ref[...])
    gi = _sigmoid(jnp.concatenate(rx, axis=1) + bx_ref[...])

    nl = -lam_ref[...]
    softplus = jnp.maximum(nl, 0.0) + jnp.log1p(jnp.exp(-jnp.abs(nl)))
    log_a = (-LRU_C) * r * softplus
    a = jnp.exp(log_a)
    th = jnp.tanh(log_a)
    b = jnp.sqrt(-2.0 * th / (1.0 - th)) * (gi * xc)

    rows = lax.broadcasted_iota(jnp.int32, (tc, 1), 0)
    d = 1
    while d < tc:
        keep = rows >= d
        a_sh = jnp.where(keep, pltpu.roll(a, d, 0), 1.0)
        b_sh = jnp.where(keep, pltpu.roll(b, d, 0), 0.0)
        b = a * b_sh + b
        a = a * a_sh
        d *= 2
    h = b + a * hcar[0:1, :]
    hcar[...] = jnp.broadcast_to(h[tc - 1:tc, :], hcar.shape)

    y = h * _gelu_tanh(yl_ref[...])
    ms = jnp.mean(y * y, axis=-1, keepdims=True)
    o_ref[...] = (y * lax.rsqrt(ms + NORM_EPS) * g_ref[...]).astype(BF16)


def _lru(xl, yl, conv_w, conv_b, wa, ba, wx, bx, lam, g, batch, seq, tc=256):
    t, lw = xl.shape
    nb = seq // tc
    row = lambda b, j: (b * nb + j, 0)
    c2 = lambda b, j: (0, 0)
    c3 = lambda b, j: (0, 0, 0)
    blk = lw // LRU_BLOCKS
    return pl.pallas_call(
        functools.partial(_lru_kernel, tc=tc),
        grid=(batch, nb),
        in_specs=[pl.BlockSpec((tc, lw), row), pl.BlockSpec((tc, lw), row),
                  pl.BlockSpec((CONV_WIDTH, lw), c2), pl.BlockSpec((1, lw), c2),
                  pl.BlockSpec((LRU_BLOCKS, blk, blk), c3), pl.BlockSpec((1, lw), c2),
                  pl.BlockSpec((LRU_BLOCKS, blk, blk), c3), pl.BlockSpec((1, lw), c2),
                  pl.BlockSpec((1, lw), c2), pl.BlockSpec((1, lw), c2)],
        out_specs=pl.BlockSpec((tc, lw), row),
        out_shape=jax.ShapeDtypeStruct((t, lw), BF16),
        scratch_shapes=[pltpu.VMEM((tc + SUBLANES, lw), F32), pltpu.VMEM((SUBLANES, lw), F32)],
        compiler_params=_params("arbitrary", "arbitrary"),
        name="lru",
    )(xl, yl, conv_w, conv_b, wa, ba, wx, bx, lam, g)


def _head_norm_rope(x, g, cos, sin, bd):
    xx = x * x
    hi, lo = _split(xx)
    ss = _dot(hi, bd) + _dot(lo, bd)
    xn = x * lax.rsqrt(ss * (1.0 / HEAD_DIM) + NORM_EPS) * g
    lane = lax.broadcasted_iota(jnp.int32, (1, LANES), 1) % HEAD_DIM
    half = ROPE_DIM // 2
    sw = jnp.where(lane < half, pltpu.roll(xn, LANES - half, 1), pltpu.roll(xn, half, 1))
    return xn * cos + sw * sin


def _attn_kernel(sink_ref, q_ref, kvc_ref, kvp_ref, cosc_ref, sinc_ref, cosp_ref, sinp_ref,
                 qg_ref, kg_ref, og_ref, o_ref, *, qb):
    first = pl.program_id(1) == 0
    nsub = qb // WINDOW
    qw = q_ref.shape[1]
    npair = qw // LANES
    pairs_per_group = npair // KV_HEADS

    li = lax.broadcasted_iota(jnp.int32, (LANES, LANES), 0) // HEAD_DIM
    lj = lax.broadcasted_iota(jnp.int32, (LANES, LANES), 1) // HEAD_DIM
    bd = (li == lj).astype(BF16)
    lane = lax.broadcasted_iota(jnp.int32, (1, LANES), 1)
    lo_half = lane < HEAD_DIM

    cosc, sinc = cosc_ref[...], sinc_ref[...]
    kc = _head_norm_rope(kvc_ref[:, 0:LANES], kg_ref[...], cosc, sinc, bd)
    kp = _head_norm_rope(kvp_ref[:, 0:LANES], kg_ref[...], cosp_ref[...], sinp_ref[...], bd)
    kfull = jnp.concatenate([kp, kc], axis=0)
    vfull = jnp.concatenate([kvp_ref[:, LANES:2 * LANES], kvc_ref[:, LANES:2 * LANES]], axis=0)

    def variants(t):
        a0 = jnp.where(lo_half, t, 0.0)
        b1 = jnp.where(lo_half, 0.0, t)
        b0 = pltpu.roll(a0, HEAD_DIM, 1)
        a1 = pltpu.roll(b1, HEAD_DIM, 1)
        return ((a0.astype(BF16), b0.astype(BF16)), (a1.astype(BF16), b1.astype(BF16)))

    kvar = variants(kfull)
    vvar = variants(vfull)

    qi = lax.broadcasted_iota(jnp.int32, (WINDOW, 2 * WINDOW), 0)
    kj = lax.broadcasted_iota(jnp.int32, (WINDOW, 2 * WINDOW), 1)
    band = (kj > qi) & (kj <= qi + WINDOW)
    bias_inner = jnp.where(band, 0.0, NEG_BIG)
    bias_first = jnp.where(band & (kj >= WINDOW), 0.0, NEG_BIG)
    bias0 = jnp.where(first, bias_first, bias_inner)

    scale = HEAD_DIM ** -0.5
    for s in range(nsub):
        r0 = s * WINDOW
        bias = bias0 if s == 0 else bias_inner
        outs = []
        for j in range(npair):
            g = j // pairs_per_group
            qp = _head_norm_rope(q_ref[r0:r0 + WINDOW, j * LANES:(j + 1) * LANES], qg_ref[...],
                                 cosc[r0:r0 + WINDOW], sinc[r0:r0 + WINDOW], bd)
            qp = (qp * scale).astype(BF16)
            acc = None
            for e in range(2):
                kk = kvar[g][e][r0:r0 + 2 * WINDOW]
                vv = vvar[g][e][r0:r0 + 2 * WINDOW]
                sc = _dot_nt(qp, kk) + bias
                sink = sink_ref[2 * j + e]
                m = jnp.maximum(jnp.max(sc, axis=-1, keepdims=True), sink)
                p = jnp.exp(sc - m)
                denom = jnp.sum(p, axis=-1, keepdims=True) + jnp.exp(sink - m)
                o = _dot(p.astype(BF16), vv) * (1.0 / denom)
                acc = o if acc is None else acc + o
            outs.append(acc)
        y = jnp.concatenate(outs, axis=1)
        ms = jnp.mean(y * y, axis=-1, keepdims=True)
        o_ref[r0:r0 + WINDOW, :] = (y * lax.rsqrt(ms + NORM_EPS) * og_ref[...]).astype(BF16)


def _attn(sinks, q, kv, cos_t, sin_t, qg, kg, og, batch, seq, qb=512):
    t, qw = q.shape
    kvw = kv.shape[1]
    nb = seq // qb
    sub = qb // WINDOW
    cur = lambda b, i: (b * nb + i, 0)
    prev = lambda b, i: (jnp.maximum((b * nb + i) * sub - 1, 0), 0)
    c2 = lambda b, i: (0, 0)
    return pl.pallas_call(
        functools.partial(_attn_kernel, qb=qb),
        grid=(batch, nb),
        in_specs=[pl.BlockSpec(memory_space=pltpu.SMEM),
                  pl.BlockSpec((qb, qw), cur),
                  pl.BlockSpec((qb, kvw), cur), pl.BlockSpec((WINDOW, kvw), prev),
                  pl.BlockSpec((qb, LANES), cur), pl.BlockSpec((qb, LANES), cur),
                  pl.BlockSpec((WINDOW, LANES), prev), pl.BlockSpec((WINDOW, LANES), prev),
                  pl.BlockSpec((1, LANES), c2), pl.BlockSpec((1, LANES), c2),
                  pl.BlockSpec((1, qw), c2)],
        out_specs=pl.BlockSpec((qb, qw), cur),
        out_shape=jax.ShapeDtypeStruct((t, qw), BF16),
        compiler_params=_params("arbitrary", "arbitrary"),
        name="attn",
    )(sinks, q, kv, kv, cos_t, sin_t, cos_t, sin_t, qg, kg, og)


def _outproj_kernel(lru_ref, att_ref, x_ref, w_ref, g1_ref, ng_ref, sh_ref, sc_ref, x1_ref, h2_ref):
    lw = lru_ref.shape[1]
    mix = _dot(lru_ref[...], w_ref[0:lw, :]) + _dot(att_ref[...], w_ref[lw:, :])
    x1 = x_ref[...] + g1_ref[0] * mix
    x1_ref[...] = x1
    ms = jnp.mean(x1 * x1, axis=-1, keepdims=True)
    xn = x1 * lax.rsqrt(ms + NORM_EPS) * ng_ref[...]
    h2_ref[...] = (xn * (1.0 + sc_ref[0]) + sh_ref[0]).astype(BF16)


def _outproj(lru_n, att_n, x2, w_out, g1, norm_g, sh, sc, seq, tm=256):
    t, d = x2.shape
    lw = lru_n.shape[1]
    aw = att_n.shape[1]
    bs = seq // tm
    row = lambda i: (i, 0)
    vec = lambda i: (i // bs, 0, 0)
    c2 = lambda i: (0, 0)
    return pl.pallas_call(
        _outproj_kernel,
        grid=(t // tm,),
        in_specs=[pl.BlockSpec((tm, lw), row), pl.BlockSpec((tm, aw), row), pl.BlockSpec((tm, d), row),
                  pl.BlockSpec((lw + aw, d), c2), pl.BlockSpec((1, 1, d), vec), pl.BlockSpec((1, d), c2),
                  pl.BlockSpec((1, 1, d), vec), pl.BlockSpec((1, 1, d), vec)],
        out_specs=[pl.BlockSpec((tm, d), row), pl.BlockSpec((tm, d), row)],
        out_shape=[jax.ShapeDtypeStruct((t, d), F32), jax.ShapeDtypeStruct((t, d), BF16)],
        compiler_params=_params("arbitrary"),
        name="outproj",
    )(lru_n, att_n, x2, w_out, g1, norm_g, sh, sc)


def _top16(s):
    r = s.shape[0]
    rowid = lax.broadcasted_iota(jnp.int32, s.shape, 0).astype(F32)
    rank = jnp.full(s.shape, float(PEER_TOPK), F32)
    vals = []
    for k in range(PEER_TOPK):
        m = jnp.max(s, axis=0, keepdims=True)
        first = jnp.min(jnp.where(s == m, rowid, float(r)), axis=0, keepdims=True)
        sel = rowid == first
        rank = jnp.where(sel, float(k), rank)
        s = jnp.where(sel, -jnp.inf, s)
        vals.append(m)
    return rank, vals


_CAND_LIMS = [PEER_TOPK // (k + 1) for k in range(PEER_TOPK)]


def _route_tile(s1, s2):
    rank1, v1 = _top16(s1)
    rank2, v2 = _top16(s2)
    v2s = jnp.concatenate(v2, axis=0)
    v1s = jnp.concatenate(v1, axis=0)
    sub = lax.broadcasted_iota(jnp.int32, (SUBLANES, 1), 0)
    pieces = [v1[0] + v2s]
    for k1 in range(1, SUBLANES):
        pieces.append(jnp.where(sub < _CAND_LIMS[k1], v1[k1] + v2s[0:SUBLANES], -jnp.inf))
    pieces.append(v1s[SUBLANES:] + v2[0])
    cand = jnp.concatenate(pieces, axis=0)
    nrow = cand.shape[0]
    rowid = lax.broadcasted_iota(jnp.int32, cand.shape, 0).astype(F32)
    cmax = v1[0] + v2[0]
    ex = jnp.exp(cand - cmax)
    selected = jnp.zeros(cand.shape, F32)
    c = cand
    for _ in range(PEER_TOPK):
        m = jnp.max(c, axis=0, keepdims=True)
        first = jnp.min(jnp.where(c == m, rowid, float(nrow)), axis=0, keepdims=True)
        sel = rowid == first
        selected = jnp.where(sel, 1.0, selected)
        c = jnp.where(sel, -jnp.inf, c)
    z = jnp.sum(selected * ex, axis=0, keepdims=True)
    counts = [jnp.sum(selected[0:2 * SUBLANES], axis=0, keepdims=True)]
    for k1 in range(1, SUBLANES):
        off = (k1 + 1) * SUBLANES
        counts.append(jnp.sum(selected[off:off + SUBLANES], axis=0, keepdims=True))
    tail = selected[nrow - SUBLANES:nrow]
    for k1 in range(SUBLANES, PEER_TOPK):
        counts.append(tail[k1 - SUBLANES:k1 - SUBLANES + 1])
    n1 = jnp.zeros(s1.shape, F32)
    for k1 in range(PEER_TOPK):
        n1 = jnp.where(rank1 == float(k1), counts[k1], n1)
    c1 = jnp.exp(s1 - v1[0]) * (1.0 / z)
    e2 = jnp.exp(s2 - v2[0])
    return n1, c1, rank2, e2


def _route_kernel(h2_ref, wq_ref, keys_ref, n1_ref, c1_ref, r2_ref, e2_ref, *, tq):
    half = PEER_KEYS

    def per_head(h, carry):
        qh = _dot(h2_ref[...], wq_ref[h])
        s1 = _dot_nt(keys_ref[h, 0], qh[:, 0:half].astype(BF16))
        s2 = _dot_nt(keys_ref[h, 1], qh[:, half:2 * half].astype(BF16))
        for lt in range(tq // LANES):
            sl = slice(lt * LANES, (lt + 1) * LANES)
            n1, c1, r2, e2 = _route_tile(s1[:, sl], s2[:, sl])
            n1_ref[h, :, sl] = n1
            c1_ref[h, :, sl] = c1
            r2_ref[h, :, sl] = r2
            e2_ref[h, :, sl] = e2
        return carry

    lax.fori_loop(0, PEER_HEADS, per_head, 0)


def _route(h2, wq3, keys, tq=256):
    t, d = h2.shape
    nh, _, qd = wq3.shape
    c3 = lambda i: (0, 0, 0)
    c4 = lambda i: (0, 0, 0, 0)
    tab = pl.BlockSpec((nh, PEER_KEYS, tq), lambda i: (0, 0, i))
    shp = jax.ShapeDtypeStruct((nh, PEER_KEYS, t), F32)
    return pl.pallas_call(
        functools.partial(_route_kernel, tq=tq),
        grid=(t // tq,),
        in_specs=[pl.BlockSpec((tq, d), lambda i: (i, 0)),
                  pl.BlockSpec((nh, d, qd), c3),
                  pl.BlockSpec(keys.shape, c4)],
        out_specs=[tab, tab, tab, tab],
        out_shape=[shp, shp, shp, shp],
        compiler_params=_params("arbitrary"),
        name="route",
    )(h2, wq3, keys)


def _experts_kernel(h2_ref, u_ref, v_ref, n1_ref, c1_ref, r2_ref, e2_ref, x1_ref, g2_ref, o_ref, *, tb, eb):
    e = pl.program_id(1)
    nsub = eb // PEER_KEYS

    @pl.when(e == 0)
    def _():
        o_ref[...] = x1_ref[...]

    act = _dot_nt(u_ref[...], h2_ref[...])
    parts = []
    for j in range(nsub):
        cols = []
        for lt in range(tb // LANES):
            sl = slice(lt * LANES, (lt + 1) * LANES)
            w = jnp.zeros((PEER_KEYS, LANES), F32)
            for h in range(PEER_HEADS):
                n1row = n1_ref[h, j:j + 1, sl]
                c1row = c1_ref[h, j:j + 1, sl]
                w = w + jnp.where(r2_ref[h, :, sl] < n1row, e2_ref[h, :, sl] * c1row, 0.0)
            a = act[j * PEER_KEYS:(j + 1) * PEER_KEYS, sl]
            cols.append(_gelu_tanh(a) * w)
        parts.append(jnp.concatenate(cols, axis=1))
    p = jnp.concatenate(parts, axis=0)
    o_ref[...] += g2_ref[0] * _dot(p.T.astype(BF16), v_ref[...])


def _experts(h2, u, v, n1, c1, r2, e2, x1, g2, seq, tb=512, eb=1024):
    t, d = h2.shape
    ne = u.shape[0]
    nh = n1.shape[0]
    nsub = eb // PEER_KEYS
    bs = seq // tb
    tok = lambda i, e: (i, 0)
    exp_ = lambda i, e: (e, 0)
    tab1 = pl.BlockSpec((nh, nsub, tb), lambda i, e: (0, e, i))
    tab2 = pl.BlockSpec((nh, PEER_KEYS, tb), lambda i, e: (0, 0, i))
    return pl.pallas_call(
        functools.partial(_experts_kernel, tb=tb, eb=eb),
        grid=(t // tb, ne // eb),
        in_specs=[pl.BlockSpec((tb, d), tok), pl.BlockSpec((eb, d), exp_), pl.BlockSpec((eb, d), exp_),
                  tab1, tab1, tab2, tab2,
                  pl.BlockSpec((tb, d), tok), pl.BlockSpec((1, 1, d), lambda i, e: (i // bs, 0, 0))],
        out_specs=pl.BlockSpec((tb, d), tok),
        out_shape=jax.ShapeDtypeStruct((t, d), F32),
        compiler_params=_params("arbitrary", "arbitrary"),
        name="experts",
    )(h2, u, v, n1, c1, r2, e2, x1, g2)


def _rope_tables(positions):
    half = ROPE_DIM // 2
    inv_freq = jnp.power(ROPE_THETA, -2.0 * jnp.arange(half, dtype=F32) / ROPE_DIM)
    ang = positions.astype(F32).reshape(-1, 1) * inv_freq
    cos, sin = jnp.cos(ang), jnp.sin(ang)
    n = ang.shape[0]
    pad = HEAD_DIM - ROPE_DIM
    cos_h = jnp.concatenate([cos, cos, jnp.ones((n, pad), F32)], axis=1)
    sin_h = jnp.concatenate([-sin, sin, jnp.zeros((n, pad), F32)], axis=1)
    reps = LANES // HEAD_DIM
    return jnp.tile(cos_h, (1, reps)), jnp.tile(sin_h, (1, reps))


def _layer(x, c, positions, w_ada, b_ada, norm1_g, w_in, conv_w, conv_b, gate_a_w, gate_a_b,
           gate_x_w, gate_x_b, lam, q_norm_g, k_norm_g, sinks, out_norm_lru_g, out_norm_attn_g,
           w_out, norm2_g, peer_w_q, peer_sub_keys, peer_u, peer_v):
    b, s, d = x.shape
    t = b * s
    lw = conv_w.shape[1]
    qw = out_norm_attn_g.shape[0]
    x2 = x.reshape(t, d)
    row = lambda a: a.reshape(1, -1)

    c_pad = jnp.zeros((SUBLANES, d), F32).at[:b].set(c)
    ada = _ada(c_pad, w_ada, row(b_ada))[:b]
    sh1, sc1, g1, sh2, sc2, g2 = [a.reshape(b, 1, d) for a in jnp.split(ada, 6, axis=-1)]

    xl, yl, q, kv = _inproj(x2, row(norm1_g), sh1, sc1, w_in.astype(BF16), s, lw, qw)
    lru_n = _lru(xl, yl, conv_w, row(conv_b), gate_a_w.astype(BF16), row(gate_a_b),
                 gate_x_w.astype(BF16), row(gate_x_b), row(lam), row(out_norm_lru_g), b, s)
    cos_t, sin_t = _rope_tables(positions)
    reps = LANES // HEAD_DIM
    att_n = _attn(sinks, q, kv, cos_t, sin_t, row(jnp.tile(q_norm_g, reps)), row(jnp.tile(k_norm_g, reps)),
                  row(out_norm_attn_g), b, s)
    x1, h2 = _outproj(lru_n, att_n, x2, w_out.astype(BF16), g1, row(norm2_g), sh2, sc2, s)

    nh = peer_sub_keys.shape[0]
    wq3 = peer_w_q.astype(BF16).reshape(d, nh, -1).transpose(1, 0, 2)
    n1, c1, r2, e2 = _route(h2, wq3, peer_sub_keys.astype(BF16))
    out = _experts(h2, peer_u.astype(BF16), peer_v.astype(BF16), n1, c1, r2, e2, x1, g2, s)
    return out.reshape(b, s, d)


def kernel(x, c, positions, w_ada, b_ada, norm1_g, w_in, conv_w, conv_b, lru_gate_a_w, lru_gate_a_b,
           lru_gate_x_w, lru_gate_x_b, lru_lambda, q_norm_g, k_norm_g, attn_sinks, out_norm_lru_g,
           out_norm_attn_g, w_out, norm2_g, peer_w_q, peer_sub_keys, peer_u, peer_v):
    for l in range(w_ada.shape[0]):
        x = _layer(x, c, positions, w_ada[l], b_ada[l], norm1_g[l], w_in[l], conv_w[l], conv_b[l],
                   lru_gate_a_w[l], lru_gate_a_b[l], lru_gate_x_w[l], lru_gate_x_b[l], lru_lambda[l],
                   q_norm_g[l], k_norm_g[l], attn_sinks[l], out_norm_lru_g[l], out_norm_attn_g[l],
                   w_out[l], norm2_g[l], peer_w_q[l], peer_sub_keys[l], peer_u[l], peer_v[l])
    return x
```

```python
import functools
import math

import jax
import jax.numpy as jnp
from jax import lax
from jax.experimental import pallas as pl
from jax.experimental.pallas import tpu as pltpu

F32 = jnp.float32
BF16 = jnp.bfloat16

LRU_BLOCKS = 8
CONV_WIDTH = 4
LRU_C = 8.0
HEAD_DIM = 64
KV_HEADS = 2
WINDOW = 128
ROPE_THETA = 500000.0
ROPE_DIM = HEAD_DIM // 4
PEER_HEADS = 8
PEER_KEYS = 128
PEER_TOPK = 16
NORM_EPS = 1e-6

LANES = 128
SUBLANES = 8
VMEM_LIMIT = 56 * 1024 * 1024
WTILE = 2 * LANES

NEG_BIG = -1e30


def _dot(a, b):
    return jnp.dot(a, b, preferred_element_type=F32)


def _dot_nt(a, b):
    return lax.dot_general(a, b, (((1,), (1,)), ((), ())), preferred_element_type=F32)


def _split(x):
    hi = x.astype(BF16)
    lo = (x - hi.astype(F32)).astype(BF16)
    return hi, lo


def _dot3(a, b):
    ah, al = _split(a)
    bh, bl = _split(b)
    return _dot(ah, bh) + (_dot(ah, bl) + _dot(al, bh))


def _sigmoid(x):
    return 1.0 / (1.0 + jnp.exp(-x))


def _gelu_tanh(x):
    c = math.sqrt(2.0 / math.pi)
    return 0.5 * x * (1.0 + jnp.tanh(c * (x + 0.044715 * (x * x * x))))


def _params(*sem, flags=None):
    return pltpu.CompilerParams(dimension_semantics=sem, vmem_limit_bytes=VMEM_LIMIT, flags=flags)


def _ada_kernel(c_ref, w_ref, b_ref, o_ref):
    c = c_ref[...]
    s = c * _sigmoid(c)
    o_ref[...] = _dot3(s, w_ref[...]) + b_ref[...]


def _ada(c_pad, w_ada, b_ada, tn=1024):
    m, d = c_pad.shape
    n = w_ada.shape[1]
    return pl.pallas_call(
        _ada_kernel,
        grid=(n // tn,),
        in_specs=[pl.BlockSpec((m, d), lambda j: (0, 0)),
                  pl.BlockSpec((d, tn), lambda j: (0, j)),
                  pl.BlockSpec((1, tn), lambda j: (0, j))],
        out_specs=pl.BlockSpec((m, tn), lambda j: (0, j)),
        out_shape=jax.ShapeDtypeStruct((m, n), F32),
        compiler_params=_params("arbitrary"),
        name="ada",
    )(c_pad, w_ada, b_ada)


def _inproj_kernel(x_ref, g_ref, sh_ref, sc_ref, w_ref, xl_ref, yl_ref, q_ref, kv_ref, *, lw, qw):
    x = x_ref[...]
    ms = jnp.mean(x * x, axis=-1, keepdims=True)
    xn = x * lax.rsqrt(ms + NORM_EPS) * g_ref[...]
    h = (xn * (1.0 + sc_ref[0]) + sh_ref[0]).astype(BF16)
    xl_ref[...] = _dot(h, w_ref[:, 0:lw])
    yl_ref[...] = _dot(h, w_ref[:, lw:2 * lw])
    q_ref[...] = _dot(h, w_ref[:, 2 * lw:2 * lw + qw])
    kv_ref[...] = _dot(h, w_ref[:, 2 * lw + qw:])


def _inproj(x2, norm_g, sh, sc, w_in, seq, lw, qw, tm=256):
    t, d = x2.shape
    n = w_in.shape[1]
    kvw = n - 2 * lw - qw
    bs = seq // tm
    row = lambda i: (i, 0)
    vec = lambda i: (i // bs, 0, 0)
    return pl.pallas_call(
        functools.partial(_inproj_kernel, lw=lw, qw=qw),
        grid=(t // tm,),
        in_specs=[pl.BlockSpec((tm, d), row),
                  pl.BlockSpec((1, d), lambda i: (0, 0)),
                  pl.BlockSpec((1, 1, d), vec),
                  pl.BlockSpec((1, 1, d), vec),
                  pl.BlockSpec((d, n), lambda i: (0, 0))],
        out_specs=[pl.BlockSpec((tm, lw), row), pl.BlockSpec((tm, lw), row),
                   pl.BlockSpec((tm, qw), row), pl.BlockSpec((tm, kvw), row)],
        out_shape=[jax.ShapeDtypeStruct((t, lw), F32), jax.ShapeDtypeStruct((t, lw), F32),
                   jax.ShapeDtypeStruct((t, qw), F32), jax.ShapeDtypeStruct((t, kvw), F32)],
        compiler_params=_params("arbitrary"),
        name="inproj",
    )(x2, norm_g, sh, sc, w_in)


def _lru_kernel(xl_ref, yl_ref, cw_ref, cb_ref, wa_ref, ba_ref, wx_ref, bx_ref, lam_ref, g_ref,
                o_ref, xext, hcar, *, tc):
    lw = xl_ref.shape[1]
    blk = lw // LRU_BLOCKS

    @pl.when(pl.program_id(1) == 0)
    def _():
        xext[0:SUBLANES, :] = jnp.zeros((SUBLANES, lw), F32)
        hcar[...] = jnp.zeros_like(hcar)

    x = xl_ref[...]
    xext[SUBLANES:SUBLANES + tc, :] = x
    xc = cb_ref[...] + cw_ref[CONV_WIDTH - 1:CONV_WIDTH, :] * x
    for k in range(CONV_WIDTH - 1):
        off = SUBLANES - (CONV_WIDTH - 1) + k
        xc = xc + cw_ref[k:k + 1, :] * xext[off:off + tc, :]
    xext[0:SUBLANES, :] = x[tc - SUBLANES:tc, :]

    ra, rx = [], []
    for hb in range(LRU_BLOCKS):
        xb = xc[:, hb * blk:(hb + 1) * blk].astype(BF16)
        ra.append(_dot(xb, wa_ref[hb]))
        rx.append(_dot(xb, wx_ref[hb]))
    r = _sigmoid(jnp.concatenate(ra, axis=1) + ba_ref[...])
    gi = _sigmoid(jnp.concatenate(rx, axis=1) + bx_ref[...])

    nl = -lam_ref[...]
    softplus = jnp.maximum(nl, 0.0) + jnp.log1p(jnp.exp(-jnp.abs(nl)))
    log_a = (-LRU_C) * r * softplus
    a = jnp.exp(log_a)
    th = jnp.tanh(log_a)
    b = jnp.sqrt(-2.0 * th / (1.0 - th)) * (gi * xc)

    rows = lax.broadcasted_iota(jnp.int32, (tc, 1), 0)
    d = 1
    while d < tc:
        keep = rows >= d
        a_sh = jnp.where(keep, pltpu.roll(a, d, 0), 1.0)
        b_sh = jnp.where(keep, pltpu.roll(b, d, 0), 0.0)
        b = a * b_sh + b
        a = a * a_sh
        d *= 2
    h = b + a * hcar[0:1, :]
    hcar[...] = jnp.broadcast_to(h[tc - 1:tc, :], hcar.shape)

    y = h * _gelu_tanh(yl_ref[...])
    ms = jnp.mean(y * y, axis=-1, keepdims=True)
    o_ref[...] = (y * lax.rsqrt(ms + NORM_EPS) * g_ref[...]).astype(BF16)


def _lru(xl, yl, conv_w, conv_b, wa, ba, wx, bx, lam, g, batch, seq, tc=256):
    t, lw = xl.shape
    nb = seq // tc
    row = lambda b, j: (b * nb + j, 0)
    c2 = lambda b, j: (0, 0)
    c3 = lambda b, j: (0, 0, 0)
    blk = lw // LRU_BLOCKS
    return pl.pallas_call(
        functools.partial(_lru_kernel, tc=tc),
        grid=(batch, nb),
        in_specs=[pl.BlockSpec((tc, lw), row), pl.BlockSpec((tc, lw), row),
                  pl.BlockSpec((CONV_WIDTH, lw), c2), pl.BlockSpec((1, lw), c2),
                  pl.BlockSpec((LRU_BLOCKS, blk, blk), c3), pl.BlockSpec((1, lw), c2),
                  pl.BlockSpec((LRU_BLOCKS, blk, blk), c3), pl.BlockSpec((1, lw), c2),
                  pl.BlockSpec((1, lw), c2), pl.BlockSpec((1, lw), c2)],
        out_specs=pl.BlockSpec((tc, lw), row),
        out_shape=jax.ShapeDtypeStruct((t, lw), BF16),
        scratch_shapes=[pltpu.VMEM((tc + SUBLANES, lw), F32), pltpu.VMEM((SUBLANES, lw), F32)],
        compiler_params=_params("arbitrary", "arbitrary"),
        name="lru",
    )(xl, yl, conv_w, conv_b, wa, ba, wx, bx, lam, g)


def _head_norm_rope(x, g, cos, sin, bd):
    xx = x * x
    hi, lo = _split(xx)
    ss = _dot(hi, bd) + _dot(lo, bd)
    xn = x * lax.rsqrt(ss * (1.0 / HEAD_DIM) + NORM_EPS) * g
    lane = lax.broadcasted_iota(jnp.int32, (1, LANES), 1) % HEAD_DIM
    half = ROPE_DIM // 2
    sw = jnp.where(lane < half, pltpu.roll(xn, LANES - half, 1), pltpu.roll(xn, half, 1))
    return xn * cos + sw * sin


def _attn_kernel(sink_ref, q_ref, kvc_ref, kvp_ref, cosc_ref, sinc_ref, cosp_ref, sinp_ref,
                 qg_ref, kg_ref, og_ref, o_ref, *, qb):
    first = pl.program_id(1) == 0
    nsub = qb // WINDOW
    qw = q_ref.shape[1]
    npair = qw // LANES
    pairs_per_group = npair // KV_HEADS

    li = lax.broadcasted_iota(jnp.int32, (LANES, LANES), 0) // HEAD_DIM
    lj = lax.broadcasted_iota(jnp.int32, (LANES, LANES), 1) // HEAD_DIM
    bd = (li == lj).astype(BF16)
    lane = lax.broadcasted_iota(jnp.int32, (1, LANES), 1)
    lo_half = lane < HEAD_DIM

    cosc, sinc = cosc_ref[...], sinc_ref[...]
    kc = _head_norm_rope(kvc_ref[:, 0:LANES], kg_ref[...], cosc, sinc, bd)
    kp = _head_norm_rope(kvp_ref[:, 0:LANES], kg_ref[...], cosp_ref[...], sinp_ref[...], bd)
    kfull = jnp.concatenate([kp, kc], axis=0)
    vfull = jnp.concatenate([kvp_ref[:, LANES:2 * LANES], kvc_ref[:, LANES:2 * LANES]], axis=0)

    def variants(t):
        a0 = jnp.where(lo_half, t, 0.0)
        b1 = jnp.where(lo_half, 0.0, t)
        b0 = pltpu.roll(a0, HEAD_DIM, 1)
        a1 = pltpu.roll(b1, HEAD_DIM, 1)
        return ((a0.astype(BF16), b0.astype(BF16)), (a1.astype(BF16), b1.astype(BF16)))

    kvar = variants(kfull)
    vvar = variants(vfull)

    qi = lax.broadcasted_iota(jnp.int32, (WINDOW, 2 * WINDOW), 0)
    kj = lax.broadcasted_iota(jnp.int32, (WINDOW, 2 * WINDOW), 1)
    band = (kj > qi) & (kj <= qi + WINDOW)
    bias_inner = jnp.where(band, 0.0, NEG_BIG)
    bias_first = jnp.where(band & (kj >= WINDOW), 0.0, NEG_BIG)
    bias0 = jnp.where(first, bias_first, bias_inner)

    scale = HEAD_DIM ** -0.5
    for s in range(nsub):
        r0 = s * WINDOW
        bias = bias0 if s == 0 else bias_inner
        outs = []
        for j in range(npair):
            g = j // pairs_per_group
            qp = _head_norm_rope(q_ref[r0:r0 + WINDOW, j * LANES:(j + 1) * LANES], qg_ref[...],
                                 cosc[r0:r0 + WINDOW], sinc[r0:r0 + WINDOW], bd)
            qp = (qp * scale).astype(BF16)
            acc = None
            for e in range(2):
                kk = kvar[g][e][r0:r0 + 2 * WINDOW]
                vv = vvar[g][e][r0:r0 + 2 * WINDOW]
                sc = _dot_nt(qp, kk) + bias
                sink = sink_ref[2 * j + e]
                m = jnp.maximum(jnp.max(sc, axis=-1, keepdims=True), sink)
                p = jnp.exp(sc - m)
                denom = jnp.sum(p, axis=-1, keepdims=True) + jnp.exp(sink - m)
                o = _dot(p.astype(BF16), vv) * (1.0 / denom)
                acc = o if acc is None else acc + o
            outs.append(acc)
        y = jnp.concatenate(outs, axis=1)
        ms = jnp.mean(y * y, axis=-1, keepdims=True)
        o_ref[r0:r0 + WINDOW, :] = (y * lax.rsqrt(ms + NORM_EPS) * og_ref[...]).astype(BF16)


def _attn(sinks, q, kv, cos_t, sin_t, qg, kg, og, batch, seq, qb=512):
    t, qw = q.shape
    kvw = kv.shape[1]
    nb = seq // qb
    sub = qb // WINDOW
    cur = lambda b, i: (b * nb + i, 0)
    prev = lambda b, i: (jnp.maximum((b * nb + i) * sub - 1, 0), 0)
    c2 = lambda b, i: (0, 0)
    return pl.pallas_call(
        functools.partial(_attn_kernel, qb=qb),
        grid=(batch, nb),
        in_specs=[pl.BlockSpec(memory_space=pltpu.SMEM),
                  pl.BlockSpec((qb, qw), cur),
                  pl.BlockSpec((qb, kvw), cur), pl.BlockSpec((WINDOW, kvw), prev),
                  pl.BlockSpec((qb, LANES), cur), pl.BlockSpec((qb, LANES), cur),
                  pl.BlockSpec((WINDOW, LANES), prev), pl.BlockSpec((WINDOW, LANES), prev),
                  pl.BlockSpec((1, LANES), c2), pl.BlockSpec((1, LANES), c2),
                  pl.BlockSpec((1, qw), c2)],
        out_specs=pl.BlockSpec((qb, qw), cur),
        out_shape=jax.ShapeDtypeStruct((t, qw), BF16),
        compiler_params=_params("arbitrary", "arbitrary"),
        name="attn",
    )(sinks, q, kv, kv, cos_t, sin_t, cos_t, sin_t, qg, kg, og)


def _outproj_kernel(lru_ref, att_ref, x_ref, w_ref, g1_ref, ng_ref, sh_ref, sc_ref, x1_ref, h2_ref):
    lw = lru_ref.shape[1]
    mix = _dot(lru_ref[...], w_ref[0:lw, :]) + _dot(att_ref[...], w_ref[lw:, :])
    x1 = x_ref[...] + g1_ref[0] * mix
    x1_ref[...] = x1
    ms = jnp.mean(x1 * x1, axis=-1, keepdims=True)
    xn = x1 * lax.rsqrt(ms + NORM_EPS) * ng_ref[...]
    h2_ref[...] = (xn * (1.0 + sc_ref[0]) + sh_ref[0]).astype(BF16)


def _outproj(lru_n, att_n, x2, w_out, g1, norm_g, sh, sc, seq, tm=256):
    t, d = x2.shape
    lw = lru_n.shape[1]
    aw = att_n.shape[1]
    bs = seq // tm
    row = lambda i: (i, 0)
    vec = lambda i: (i // bs, 0, 0)
    c2 = lambda i: (0, 0)
    return pl.pallas_call(
        _outproj_kernel,
        grid=(t // tm,),
        in_specs=[pl.BlockSpec((tm, lw), row), pl.BlockSpec((tm, aw), row), pl.BlockSpec((tm, d), row),
                  pl.BlockSpec((lw + aw, d), c2), pl.BlockSpec((1, 1, d), vec), pl.BlockSpec((1, d), c2),
                  pl.BlockSpec((1, 1, d), vec), pl.BlockSpec((1, 1, d), vec)],
        out_specs=[pl.BlockSpec((tm, d), row), pl.BlockSpec((tm, d), row)],
        out_shape=[jax.ShapeDtypeStruct((t, d), F32), jax.ShapeDtypeStruct((t, d), BF16)],
        compiler_params=_params("arbitrary"),
        name="outproj",
    )(lru_n, att_n, x2, w_out, g1, norm_g, sh, sc)


def _top16(s):
    r = s.shape[0]
    rowid = lax.broadcasted_iota(jnp.int32, s.shape, 0).astype(F32)
    rank = jnp.full(s.shape, float(PEER_TOPK), F32)
    vals = []
    for k in range(PEER_TOPK):
        m = jnp.max(s, axis=0, keepdims=True)
        first = jnp.min(jnp.where(s == m, rowid, float(r)), axis=0, keepdims=True)
        sel = rowid == first
        rank = jnp.where(sel, float(k), rank)
        s = jnp.where(sel, -jnp.inf, s)
        vals.append(m)
    return rank, vals


_CAND_LIMS = [PEER_TOPK // (k + 1) for k in range(PEER_TOPK)]


def _route_tile(s1, s2):
    rank1, v1 = _top16(s1)
    rank2, v2 = _top16(s2)
    v2s = jnp.concatenate(v2, axis=0)
    v1s = jnp.concatenate(v1, axis=0)
    sub = lax.broadcasted_iota(jnp.int32, (SUBLANES, 1), 0)
    pieces = [v1[0] + v2s]
    for k1 in range(1, SUBLANES):
        pieces.append(jnp.where(sub < _CAND_LIMS[k1], v1[k1] + v2s[0:SUBLANES], -jnp.inf))
    pieces.append(v1s[SUBLANES:] + v2[0])
    cand = jnp.concatenate(pieces, axis=0)
    nrow = cand.shape[0]
    rowid = lax.broadcasted_iota(jnp.int32, cand.shape, 0).astype(F32)
    cmax = v1[0] + v2[0]
    ex = jnp.exp(cand - cmax)
    selected = jnp.zeros(cand.shape, F32)
    c = cand
    for _ in range(PEER_TOPK):
        m = jnp.max(c, axis=0, keepdims=True)
        first = jnp.min(jnp.where(c == m, rowid, float(nrow)), axis=0, keepdims=True)
        sel = rowid == first
        selected = jnp.where(sel, 1.0, selected)
        c = jnp.where(sel, -jnp.inf, c)
    z = jnp.sum(selected * ex, axis=0, keepdims=True)
    counts = [jnp.sum(selected[0:2 * SUBLANES], axis=0, keepdims=True)]
    for k1 in range(1, SUBLANES):
        off = (k1 + 1) * SUBLANES
        counts.append(jnp.sum(selected[off:off + SUBLANES], axis=0, keepdims=True))
    tail = selected[nrow - SUBLANES:nrow]
    for k1 in range(SUBLANES, PEER_TOPK):
        counts.append(tail[k1 - SUBLANES:k1 - SUBLANES + 1])
    n1 = jnp.zeros(s1.shape, F32)
    for k1 in range(PEER_TOPK):
        n1 = jnp.where(rank1 == float(k1), counts[k1], n1)
    c1 = jnp.exp(s1 - v1[0]) * (1.0 / z)
    e2 = jnp.exp(s2 - v2[0])
    return n1, c1, rank2, e2


def _route_kernel(h2_ref, wq_ref, keys_ref, n1_ref, c1_ref, r2_ref, e2_ref, *, tq):
    half = PEER_KEYS

    def per_head(h, carry):
        qh = _dot(h2_ref[...], wq_ref[h])
        s1 = _dot_nt(keys_ref[h, 0], qh[:, 0:half].astype(BF16))
        s2 = _dot_nt(keys_ref[h, 1], qh[:, half:2 * half].astype(BF16))
        r2s, e2s = [], []
        for lt in range(tq // LANES):
            sl = slice(lt * LANES, (lt + 1) * LANES)
            n1, c1, r2, e2 = _route_tile(s1[:, sl], s2[:, sl])
            n1_ref[h, :, sl] = n1
            c1_ref[h, :, sl] = c1
            r2s.append(r2)
            e2s.append(e2)
        per = WTILE // LANES
        for wt in range(tq // WTILE):
            r2_ref[h, wt] = jnp.concatenate(r2s[wt * per:(wt + 1) * per], axis=1).astype(BF16)
            e2_ref[h, wt] = jnp.concatenate(e2s[wt * per:(wt + 1) * per], axis=1).astype(BF16)
        return carry

    lax.fori_loop(0, PEER_HEADS, per_head, 0)


def _route(h2, wq3, keys, tq=256):
    t, d = h2.shape
    nh, _, qd = wq3.shape
    c3 = lambda i: (0, 0, 0)
    c4 = lambda i: (0, 0, 0, 0)
    tab = pl.BlockSpec((nh, PEER_KEYS, tq), lambda i: (0, 0, i))
    shp = jax.ShapeDtypeStruct((nh, PEER_KEYS, t), F32)
    tile = pl.BlockSpec((nh, tq // WTILE, PEER_KEYS, WTILE), lambda i: (0, i, 0, 0))
    tshp = jax.ShapeDtypeStruct((nh, t // WTILE, PEER_KEYS, WTILE), BF16)
    return pl.pallas_call(
        functools.partial(_route_kernel, tq=tq),
        grid=(t // tq,),
        in_specs=[pl.BlockSpec((tq, d), lambda i: (i, 0)),
                  pl.BlockSpec((nh, d, qd), c3),
                  pl.BlockSpec(keys.shape, c4)],
        out_specs=[tab, tab, tile, tile],
        out_shape=[shp, shp, tshp, tshp],
        compiler_params=_params("arbitrary"),
        name="route",
    )(h2, wq3, keys)


def _experts_kernel(h2_ref, u_ref, v_ref, n1_ref, c1_ref, r2_ref, e2_ref, x1_ref, g2_ref, o_ref, pt_ref,
                    w_ref, *, tb, eb):
    e = pl.program_id(1)
    nsub = eb // PEER_KEYS

    @pl.when(e == 0)
    def _():
        o_ref[...] = x1_ref[...]

    d = o_ref.shape[1]
    ntile = tb // LANES
    cw = d // nsub
    arows = 2 * PEER_KEYS

    def weights(j):
        pk = 2 * SUBLANES
        reps = PEER_KEYS // pk

        def rows(ref, h, sl):
            r = jnp.broadcast_to(ref[h, j:j + 1, sl], (pk, WTILE)).astype(BF16)
            return jnp.concatenate([r] * reps, axis=0)

        for wt in range(tb // WTILE):
            sl = slice(wt * WTILE, (wt + 1) * WTILE)
            w = jnp.zeros((PEER_KEYS, WTILE), BF16)
            for h in range(PEER_HEADS):
                e2 = e2_ref[h, wt]
                w = w + jnp.where(r2_ref[h, wt] < rows(n1_ref, h, sl), e2 * rows(c1_ref, h, sl),
                                  jnp.zeros_like(e2))
            w_ref[j, wt] = w

    def activate(c, act):
        for jj in range(arows // PEER_KEYS):
            j = c * (arows // PEER_KEYS) + jj
            for lt in range(ntile):
                sl = slice(lt * LANES, (lt + 1) * LANES)
                a = act[jj * PEER_KEYS:(jj + 1) * PEER_KEYS, sl]
                wl = (lt * LANES) % WTILE
                w = w_ref[j, lt * LANES // WTILE][:, wl:wl + LANES]
                p = _gelu_tanh(a) * w.astype(F32)
                pt_ref[sl, j * PEER_KEYS:(j + 1) * PEER_KEYS] = p.T.astype(BF16)

    @pl.when(e > 0)
    def _():
        for n in range(nsub):
            cs = slice(n * cw, (n + 1) * cw)
            o_ref[:, cs] += g2_ref[0][:, cs] * _dot(pt_ref[...], v_ref[:, cs])

    @pl.when(e < pl.num_programs(1) - 1)
    def _():
        prev_act = None
        for c in range(eb // arows):
            weights(2 * c)
            weights(2 * c + 1)
            act = _dot_nt(u_ref[c * arows:(c + 1) * arows, :], h2_ref[...])
            if prev_act is not None:
                activate(c - 1, prev_act)
            prev_act = act
        activate(eb // arows - 1, prev_act)


def _experts(h2, u, v, n1, c1, r2, e2, x1, g2, seq, tb=512, eb=1024):
    t, d = h2.shape
    ne = u.shape[0] // eb
    nh = n1.shape[0]
    nsub = eb // PEER_KEYS
    bs = seq // tb
    tok = lambda i, e: (i, 0)
    cur = lambda i, e: (jnp.minimum(e, ne - 1), 0)
    prev = lambda i, e: (jnp.maximum(e - 1, 0), 0)
    tab1 = pl.BlockSpec((nh, nsub, tb), lambda i, e: (0, jnp.minimum(e, ne - 1), i))
    tab2 = pl.BlockSpec((nh, tb // WTILE, PEER_KEYS, WTILE), lambda i, e: (0, i, 0, 0))
    return pl.pallas_call(
        functools.partial(_experts_kernel, tb=tb, eb=eb),
        grid=(t // tb, ne + 1),
        in_specs=[pl.BlockSpec((tb, d), tok), pl.BlockSpec((eb, d), cur), pl.BlockSpec((eb, d), prev),
                  tab1, tab1, tab2, tab2,
                  pl.BlockSpec((tb, d), tok), pl.BlockSpec((1, 1, d), lambda i, e: (i // bs, 0, 0))],
        out_specs=pl.BlockSpec((tb, d), tok),
        out_shape=jax.ShapeDtypeStruct((t, d), F32),
        scratch_shapes=[pltpu.VMEM((tb, eb), BF16), pltpu.VMEM((nsub, tb // WTILE, PEER_KEYS, WTILE), BF16)],
        compiler_params=_params("arbitrary", "arbitrary"),
        name="experts",
    )(h2, u, v, n1, c1, r2, e2, x1, g2)


def _rope_tables(positions):
    half = ROPE_DIM // 2
    inv_freq = jnp.power(ROPE_THETA, -2.0 * jnp.arange(half, dtype=F32) / ROPE_DIM)
    ang = positions.astype(F32).reshape(-1, 1) * inv_freq
    cos, sin = jnp.cos(ang), jnp.sin(ang)
    n = ang.shape[0]
    pad = HEAD_DIM - ROPE_DIM
    cos_h = jnp.concatenate([cos, cos, jnp.ones((n, pad), F32)], axis=1)
    sin_h = jnp.concatenate([-sin, sin, jnp.zeros((n, pad), F32)], axis=1)
    reps = LANES // HEAD_DIM
    return jnp.tile(cos_h, (1, reps)), jnp.tile(sin_h, (1, reps))


def _layer(x, c, positions, w_ada, b_ada, norm1_g, w_in, conv_w, conv_b, gate_a_w, gate_a_b,
           gate_x_w, gate_x_b, lam, q_norm_g, k_norm_g, sinks, out_norm_lru_g, out_norm_attn_g,
           w_out, norm2_g, peer_w_q, peer_sub_keys, peer_u, peer_v):
    b, s, d = x.shape
    t = b * s
    lw = conv_w.shape[1]
    qw = out_norm_attn_g.shape[0]
    x2 = x.reshape(t, d)
    row = lambda a: a.reshape(1, -1)

    c_pad = jnp.zeros((SUBLANES, d), F32).at[:b].set(c)
    ada = _ada(c_pad, w_ada, row(b_ada))[:b]
    sh1, sc1, g1, sh2, sc2, g2 = [a.reshape(b, 1, d) for a in jnp.split(ada, 6, axis=-1)]

    xl, yl, q, kv = _inproj(x2, row(norm1_g), sh1, sc1, w_in.astype(BF16), s, lw, qw)
    lru_n = _lru(xl, yl, conv_w, row(conv_b), gate_a_w.astype(BF16), row(gate_a_b),
                 gate_x_w.astype(BF16), row(gate_x_b), row(lam), row(out_norm_lru_g), b, s)
    cos_t, sin_t = _rope_tables(positions)
    reps = LANES // HEAD_DIM
    att_n = _attn(sinks, q, kv, cos_t, sin_t, row(jnp.tile(q_norm_g, reps)), row(jnp.tile(k_norm_g, reps)),
                  row(out_norm_attn_g), b, s)
    x1, h2 = _outproj(lru_n, att_n, x2, w_out.astype(BF16), g1, row(norm2_g), sh2, sc2, s)

    nh = peer_sub_keys.shape[0]
    wq3 = peer_w_q.astype(BF16).reshape(d, nh, -1).transpose(1, 0, 2)
    n1, c1, r2, e2 = _route(h2, wq3, peer_sub_keys.astype(BF16))
    out = _experts(h2, peer_u.astype(BF16), peer_v.astype(BF16), n1, c1, r2, e2, x1, g2, s)
    return out.reshape(b, s, d)


def kernel(x, c, positions, w_ada, b_ada, norm1_g, w_in, conv_w, conv_b, lru_gate_a_w, lru_gate_a_b,
           lru_gate_x_w, lru_gate_x_b, lru_lambda, q_norm_g, k_norm_g, attn_sinks, out_norm_lru_g,
           out_norm_attn_g, w_out, norm2_g, peer_w_q, peer_sub_keys, peer_u, peer_v):
    for l in range(w_ada.shape[0]):
        x = _layer(x, c, positions, w_ada[l], b_ada[l], norm1_g[l], w_in[l], conv_w[l], conv_b[l],
                   lru_gate_a_w[l], lru_gate_a_b[l], lru_gate_x_w[l], lru_gate_x_b[l], lru_lambda[l],
                   q_norm_g[l], k_norm_g[l], attn_sinks[l], out_norm_lru_g[l], out_norm_attn_g[l],
                   w_out[l], norm2_g[l], peer_w_q[l], peer_sub_keys[l], peer_u[l], peer_v[l])
    return x
```

```python
import functools
import math

import jax
import jax.numpy as jnp
from jax import lax
from jax.experimental import pallas as pl
from jax.experimental.pallas import tpu as pltpu

F32 = jnp.float32
BF16 = jnp.bfloat16

LRU_BLOCKS = 8
CONV_WIDTH = 4
LRU_C = 8.0
HEAD_DIM = 64
KV_HEADS = 2
WINDOW = 128
ROPE_THETA = 500000.0
ROPE_DIM = HEAD_DIM // 4
PEER_HEADS = 8
PEER_KEYS = 128
PEER_TOPK = 16
NORM_EPS = 1e-6

LANES = 128
SUBLANES = 8
VMEM_LIMIT = 56 * 1024 * 1024
WTILE = 2 * LANES

NEG_BIG = -1e30


def _dot(a, b):
    return jnp.dot(a, b, preferred_element_type=F32)


def _dot_nt(a, b):
    return lax.dot_general(a, b, (((1,), (1,)), ((), ())), preferred_element_type=F32)


def _split(x):
    hi = x.astype(BF16)
    lo = (x - hi.astype(F32)).astype(BF16)
    return hi, lo


def _dot3(a, b):
    ah, al = _split(a)
    bh, bl = _split(b)
    return _dot(ah, bh) + (_dot(ah, bl) + _dot(al, bh))


def _sigmoid(x):
    return 1.0 / (1.0 + jnp.exp(-x))


def _gelu_tanh(x):
    c = math.sqrt(2.0 / math.pi)
    return 0.5 * x * (1.0 + jnp.tanh(c * (x + 0.044715 * (x * x * x))))


def _params(*sem, flags=None):
    return pltpu.CompilerParams(dimension_semantics=sem, vmem_limit_bytes=VMEM_LIMIT, flags=flags)


def _ada_kernel(c_ref, w_ref, b_ref, o_ref):
    c = c_ref[...]
    s = c * _sigmoid(c)
    o_ref[...] = _dot3(s, w_ref[...]) + b_ref[...]


def _ada(c_pad, w_ada, b_ada, tn=1024):
    m, d = c_pad.shape
    n = w_ada.shape[1]
    return pl.pallas_call(
        _ada_kernel,
        grid=(n // tn,),
        in_specs=[pl.BlockSpec((m, d), lambda j: (0, 0)),
                  pl.BlockSpec((d, tn), lambda j: (0, j)),
                  pl.BlockSpec((1, tn), lambda j: (0, j))],
        out_specs=pl.BlockSpec((m, tn), lambda j: (0, j)),
        out_shape=jax.ShapeDtypeStruct((m, n), F32),
        compiler_params=_params("arbitrary"),
        name="ada",
    )(c_pad, w_ada, b_ada)


def _inproj_kernel(x_ref, g_ref, sh_ref, sc_ref, w_ref, xl_ref, yl_ref, q_ref, kv_ref, *, lw, qw):
    x = x_ref[...]
    ms = jnp.mean(x * x, axis=-1, keepdims=True)
    xn = x * lax.rsqrt(ms + NORM_EPS) * g_ref[...]
    h = (xn * (1.0 + sc_ref[0]) + sh_ref[0]).astype(BF16)
    xl_ref[...] = _dot(h, w_ref[:, 0:lw])
    yl_ref[...] = _dot(h, w_ref[:, lw:2 * lw])
    q_ref[...] = _dot(h, w_ref[:, 2 * lw:2 * lw + qw])
    kv_ref[...] = _dot(h, w_ref[:, 2 * lw + qw:])


def _inproj(x2, norm_g, sh, sc, w_in, seq, lw, qw, tm=256):
    t, d = x2.shape
    n = w_in.shape[1]
    kvw = n - 2 * lw - qw
    bs = seq // tm
    row = lambda i: (i, 0)
    vec = lambda i: (i // bs, 0, 0)
    return pl.pallas_call(
        functools.partial(_inproj_kernel, lw=lw, qw=qw),
        grid=(t // tm,),
        in_specs=[pl.BlockSpec((tm, d), row),
                  pl.BlockSpec((1, d), lambda i: (0, 0)),
                  pl.BlockSpec((1, 1, d), vec),
                  pl.BlockSpec((1, 1, d), vec),
                  pl.BlockSpec((d, n), lambda i: (0, 0))],
        out_specs=[pl.BlockSpec((tm, lw), row), pl.BlockSpec((tm, lw), row),
                   pl.BlockSpec((tm, qw), row), pl.BlockSpec((tm, kvw), row)],
        out_shape=[jax.ShapeDtypeStruct((t, lw), F32), jax.ShapeDtypeStruct((t, lw), F32),
                   jax.ShapeDtypeStruct((t, qw), F32), jax.ShapeDtypeStruct((t, kvw), F32)],
        compiler_params=_params("arbitrary"),
        name="inproj",
    )(x2, norm_g, sh, sc, w_in)


def _lru_kernel(xl_ref, yl_ref, cw_ref, cb_ref, wa_ref, ba_ref, wx_ref, bx_ref, lam_ref, g_ref,
                o_ref, xext, hcar, *, tc):
    lw = xl_ref.shape[1]
    blk = lw // LRU_BLOCKS

    @pl.when(pl.program_id(1) == 0)
    def _():
        xext[0:SUBLANES, :] = jnp.zeros((SUBLANES, lw), F32)
        hcar[...] = jnp.zeros_like(hcar)

    x = xl_ref[...]
    xext[SUBLANES:SUBLANES + tc, :] = x
    xc = cb_ref[...] + cw_ref[CONV_WIDTH - 1:CONV_WIDTH, :] * x
    for k in range(CONV_WIDTH - 1):
        off = SUBLANES - (CONV_WIDTH - 1) + k
        xc = xc + cw_ref[k:k + 1, :] * xext[off:off + tc, :]
    xext[0:SUBLANES, :] = x[tc - SUBLANES:tc, :]

    ra, rx = [], []
    for hb in range(LRU_BLOCKS):
        xb = xc[:, hb * blk:(hb + 1) * blk].astype(BF16)
        ra.append(_dot(xb, wa_ref[hb]))
        rx.append(_dot(xb, wx_ref[hb]))
    r = _sigmoid(jnp.concatenate(ra, axis=1) + ba_ref[...])
    gi = _sigmoid(jnp.concatenate(rx, axis=1) + bx_ref[...])

    nl = -lam_ref[...]
    softplus = jnp.maximum(nl, 0.0) + jnp.log1p(jnp.exp(-jnp.abs(nl)))
    log_a = (-LRU_C) * r * softplus
    a = jnp.exp(log_a)
    th = jnp.tanh(log_a)
    b = jnp.sqrt(-2.0 * th / (1.0 - th)) * (gi * xc)

    rows = lax.broadcasted_iota(jnp.int32, (tc, 1), 0)
    d = 1
    while d < tc:
        keep = rows >= d
        a_sh = jnp.where(keep, pltpu.roll(a, d, 0), 1.0)
        b_sh = jnp.where(keep, pltpu.roll(b, d, 0), 0.0)
        b = a * b_sh + b
        a = a * a_sh
        d *= 2
    h = b + a * hcar[0:1, :]
    hcar[...] = jnp.broadcast_to(h[tc - 1:tc, :], hcar.shape)

    y = h * _gelu_tanh(yl_ref[...])
    ms = jnp.mean(y * y, axis=-1, keepdims=True)
    o_ref[...] = (y * lax.rsqrt(ms + NORM_EPS) * g_ref[...]).astype(BF16)


def _lru(xl, yl, conv_w, conv_b, wa, ba, wx, bx, lam, g, batch, seq, tc=256):
    t, lw = xl.shape
    nb = seq // tc
    row = lambda b, j: (b * nb + j, 0)
    c2 = lambda b, j: (0, 0)
    c3 = lambda b, j: (0, 0, 0)
    blk = lw // LRU_BLOCKS
    return pl.pallas_call(
        functools.partial(_lru_kernel, tc=tc),
        grid=(batch, nb),
        in_specs=[pl.BlockSpec((tc, lw), row), pl.BlockSpec((tc, lw), row),
                  pl.BlockSpec((CONV_WIDTH, lw), c2), pl.BlockSpec((1, lw), c2),
                  pl.BlockSpec((LRU_BLOCKS, blk, blk), c3), pl.BlockSpec((1, lw), c2),
                  pl.BlockSpec((LRU_BLOCKS, blk, blk), c3), pl.BlockSpec((1, lw), c2),
                  pl.BlockSpec((1, lw), c2), pl.BlockSpec((1, lw), c2)],
        out_specs=pl.BlockSpec((tc, lw), row),
        out_shape=jax.ShapeDtypeStruct((t, lw), BF16),
        scratch_shapes=[pltpu.VMEM((tc + SUBLANES, lw), F32), pltpu.VMEM((SUBLANES, lw), F32)],
        compiler_params=_params("arbitrary", "arbitrary"),
        name="lru",
    )(xl, yl, conv_w, conv_b, wa, ba, wx, bx, lam, g)


def _head_norm_rope(x, g, cos, sin, bd):
    xx = x * x
    hi, lo = _split(xx)
    ss = _dot(hi, bd) + _dot(lo, bd)
    xn = x * lax.rsqrt(ss * (1.0 / HEAD_DIM) + NORM_EPS) * g
    lane = lax.broadcasted_iota(jnp.int32, (1, LANES), 1) % HEAD_DIM
    half = ROPE_DIM // 2
    sw = jnp.where(lane < half, pltpu.roll(xn, LANES - half, 1), pltpu.roll(xn, half, 1))
    return xn * cos + sw * sin


def _attn_kernel(sink_ref, q_ref, kvc_ref, kvp_ref, cosc_ref, sinc_ref, cosp_ref, sinp_ref,
                 qg_ref, kg_ref, og_ref, o_ref, *, qb):
    first = pl.program_id(1) == 0
    nsub = qb // WINDOW
    qw = q_ref.shape[1]
    npair = qw // LANES
    pairs_per_group = npair // KV_HEADS

    li = lax.broadcasted_iota(jnp.int32, (LANES, LANES), 0) // HEAD_DIM
    lj = lax.broadcasted_iota(jnp.int32, (LANES, LANES), 1) // HEAD_DIM
    bd = (li == lj).astype(BF16)
    lane = lax.broadcasted_iota(jnp.int32, (1, LANES), 1)
    lo_half = lane < HEAD_DIM

    cosc, sinc = cosc_ref[...], sinc_ref[...]
    kc = _head_norm_rope(kvc_ref[:, 0:LANES], kg_ref[...], cosc, sinc, bd)
    kp = _head_norm_rope(kvp_ref[:, 0:LANES], kg_ref[...], cosp_ref[...], sinp_ref[...], bd)
    kfull = jnp.concatenate([kp, kc], axis=0)
    vfull = jnp.concatenate([kvp_ref[:, LANES:2 * LANES], kvc_ref[:, LANES:2 * LANES]], axis=0)

    def variants(t):
        a0 = jnp.where(lo_half, t, 0.0)
        b1 = jnp.where(lo_half, 0.0, t)
        b0 = pltpu.roll(a0, HEAD_DIM, 1)
        a1 = pltpu.roll(b1, HEAD_DIM, 1)
        return ((a0.astype(BF16), b0.astype(BF16)), (a1.astype(BF16), b1.astype(BF16)))

    kvar = variants(kfull)
    vvar = variants(vfull)

    qi = lax.broadcasted_iota(jnp.int32, (WINDOW, 2 * WINDOW), 0)
    kj = lax.broadcasted_iota(jnp.int32, (WINDOW, 2 * WINDOW), 1)
    band = (kj > qi) & (kj <= qi + WINDOW)
    bias_inner = jnp.where(band, 0.0, NEG_BIG)
    bias_first = jnp.where(band & (kj >= WINDOW), 0.0, NEG_BIG)
    bias0 = jnp.where(first, bias_first, bias_inner)

    scale = HEAD_DIM ** -0.5
    for s in range(nsub):
        r0 = s * WINDOW
        bias = bias0 if s == 0 else bias_inner
        outs = []
        for j in range(npair):
            g = j // pairs_per_group
            qp = _head_norm_rope(q_ref[r0:r0 + WINDOW, j * LANES:(j + 1) * LANES], qg_ref[...],
                                 cosc[r0:r0 + WINDOW], sinc[r0:r0 + WINDOW], bd)
            qp = (qp * scale).astype(BF16)
            acc = None
            for e in range(2):
                kk = kvar[g][e][r0:r0 + 2 * WINDOW]
                vv = vvar[g][e][r0:r0 + 2 * WINDOW]
                sc = _dot_nt(qp, kk) + bias
                sink = sink_ref[2 * j + e]
                m = jnp.maximum(jnp.max(sc, axis=-1, keepdims=True), sink)
                p = jnp.exp(sc - m)
                denom = jnp.sum(p, axis=-1, keepdims=True) + jnp.exp(sink - m)
                o = _dot(p.astype(BF16), vv) * (1.0 / denom)
                acc = o if acc is None else acc + o
            outs.append(acc)
        y = jnp.concatenate(outs, axis=1)
        ms = jnp.mean(y * y, axis=-1, keepdims=True)
        o_ref[r0:r0 + WINDOW, :] = (y * lax.rsqrt(ms + NORM_EPS) * og_ref[...]).astype(BF16)


def _attn(sinks, q, kv, cos_t, sin_t, qg, kg, og, batch, seq, qb=512):
    t, qw = q.shape
    kvw = kv.shape[1]
    nb = seq // qb
    sub = qb // WINDOW
    cur = lambda b, i: (b * nb + i, 0)
    prev = lambda b, i: (jnp.maximum((b * nb + i) * sub - 1, 0), 0)
    c2 = lambda b, i: (0, 0)
    return pl.pallas_call(
        functools.partial(_attn_kernel, qb=qb),
        grid=(batch, nb),
        in_specs=[pl.BlockSpec(memory_space=pltpu.SMEM),
                  pl.BlockSpec((qb, qw), cur),
                  pl.BlockSpec((qb, kvw), cur), pl.BlockSpec((WINDOW, kvw), prev),
                  pl.BlockSpec((qb, LANES), cur), pl.BlockSpec((qb, LANES), cur),
                  pl.BlockSpec((WINDOW, LANES), prev), pl.BlockSpec((WINDOW, LANES), prev),
                  pl.BlockSpec((1, LANES), c2), pl.BlockSpec((1, LANES), c2),
                  pl.BlockSpec((1, qw), c2)],
        out_specs=pl.BlockSpec((qb, qw), cur),
        out_shape=jax.ShapeDtypeStruct((t, qw), BF16),
        compiler_params=_params("arbitrary", "arbitrary"),
        name="attn",
    )(sinks, q, kv, kv, cos_t, sin_t, cos_t, sin_t, qg, kg, og)


def _outproj_kernel(lru_ref, att_ref, x_ref, w_ref, g1_ref, ng_ref, sh_ref, sc_ref, x1_ref, h2_ref):
    lw = lru_ref.shape[1]
    mix = _dot(lru_ref[...], w_ref[0:lw, :]) + _dot(att_ref[...], w_ref[lw:, :])
    x1 = x_ref[...] + g1_ref[0] * mix
    x1_ref[...] = x1
    ms = jnp.mean(x1 * x1, axis=-1, keepdims=True)
    xn = x1 * lax.rsqrt(ms + NORM_EPS) * ng_ref[...]
    h2_ref[...] = (xn * (1.0 + sc_ref[0]) + sh_ref[0]).astype(BF16)


def _outproj(lru_n, att_n, x2, w_out, g1, norm_g, sh, sc, seq, tm=256):
    t, d = x2.shape
    lw = lru_n.shape[1]
    aw = att_n.shape[1]
    bs = seq // tm
    row = lambda i: (i, 0)
    vec = lambda i: (i // bs, 0, 0)
    c2 = lambda i: (0, 0)
    return pl.pallas_call(
        _outproj_kernel,
        grid=(t // tm,),
        in_specs=[pl.BlockSpec((tm, lw), row), pl.BlockSpec((tm, aw), row), pl.BlockSpec((tm, d), row),
                  pl.BlockSpec((lw + aw, d), c2), pl.BlockSpec((1, 1, d), vec), pl.BlockSpec((1, d), c2),
                  pl.BlockSpec((1, 1, d), vec), pl.BlockSpec((1, 1, d), vec)],
        out_specs=[pl.BlockSpec((tm, d), row), pl.BlockSpec((tm, d), row)],
        out_shape=[jax.ShapeDtypeStruct((t, d), F32), jax.ShapeDtypeStruct((t, d), BF16)],
        compiler_params=_params("arbitrary"),
        name="outproj",
    )(lru_n, att_n, x2, w_out, g1, norm_g, sh, sc)


def _top16(s, exact):
    r = s.shape[0]
    rowid = lax.broadcasted_iota(jnp.int32, s.shape, 0).astype(F32) if exact else None
    rank = jnp.full(s.shape, float(PEER_TOPK), F32)
    vals = []
    for k in range(PEER_TOPK):
        m = jnp.max(s, axis=0, keepdims=True)
        sel = s == m
        if exact:
            first = jnp.min(jnp.where(sel, rowid, float(r)), axis=0, keepdims=True)
            sel = rowid == first
        rank = jnp.where(sel, float(k), rank)
        s = jnp.where(sel, -jnp.inf, s)
        vals.append(m)
    taken = jnp.sum(jnp.where(rank < float(PEER_TOPK), 1.0, 0.0), axis=0, keepdims=True)
    return rank, vals, jnp.where(taken == float(PEER_TOPK), 0.0, 1.0)


_CAND_LIMS = [PEER_TOPK // (k + 1) for k in range(PEER_TOPK)]


def _route_tile(s1, s2, exact):
    rank1, v1, bad1 = _top16(s1, exact)
    rank2, v2, bad2 = _top16(s2, exact)
    v2s = jnp.concatenate(v2, axis=0)
    v1s = jnp.concatenate(v1, axis=0)
    sub = lax.broadcasted_iota(jnp.int32, (SUBLANES, 1), 0)
    pieces = [v1[0] + v2s]
    for k1 in range(1, SUBLANES):
        pieces.append(jnp.where(sub < _CAND_LIMS[k1], v1[k1] + v2s[0:SUBLANES], -jnp.inf))
    pieces.append(v1s[SUBLANES:] + v2[0])
    cand = jnp.concatenate(pieces, axis=0)
    nrow = cand.shape[0]
    rowid = lax.broadcasted_iota(jnp.int32, cand.shape, 0).astype(F32) if exact else None
    cmax = v1[0] + v2[0]
    ex = jnp.exp(cand - cmax)
    c = cand
    for _ in range(PEER_TOPK):
        m = jnp.max(c, axis=0, keepdims=True)
        sel = c == m
        if exact:
            first = jnp.min(jnp.where(sel, rowid, float(nrow)), axis=0, keepdims=True)
            sel = rowid == first
        c = jnp.where(sel, -jnp.inf, c)
    selected = jnp.where(c != cand, 1.0, 0.0)
    taken = jnp.sum(selected, axis=0, keepdims=True)
    bad = jnp.maximum(jnp.maximum(bad1, bad2), jnp.where(taken == float(PEER_TOPK), 0.0, 1.0))
    z = jnp.sum(selected * ex, axis=0, keepdims=True)
    counts = [jnp.sum(selected[0:2 * SUBLANES], axis=0, keepdims=True)]
    for k1 in range(1, SUBLANES):
        off = (k1 + 1) * SUBLANES
        counts.append(jnp.sum(selected[off:off + SUBLANES], axis=0, keepdims=True))
    tail = selected[nrow - SUBLANES:nrow]
    for k1 in range(SUBLANES, PEER_TOPK):
        counts.append(tail[k1 - SUBLANES:k1 - SUBLANES + 1])
    n1 = jnp.zeros(s1.shape, F32)
    for k1 in range(PEER_TOPK):
        n1 = jnp.where(rank1 == float(k1), counts[k1], n1)
    c1 = jnp.exp(s1 - v1[0]) * (1.0 / z)
    e2 = jnp.exp(s2 - v2[0])
    return n1, c1, rank2, e2, bad


def _route_kernel(h2_ref, wq_ref, keys_ref, n1_ref, c1_ref, r2_ref, e2_ref, *, tq):
    half = PEER_KEYS
    per = WTILE // LANES

    def per_head(exact, h, bad):
        qh = _dot(h2_ref[...], wq_ref[h])
        s1 = _dot_nt(keys_ref[h, 0], qh[:, 0:half].astype(BF16))
        s2 = _dot_nt(keys_ref[h, 1], qh[:, half:2 * half].astype(BF16))
        r2s, e2s, bads = [], [], []
        for lt in range(tq // LANES):
            sl = slice(lt * LANES, (lt + 1) * LANES)
            n1, c1, r2, e2, b = _route_tile(s1[:, sl], s2[:, sl], exact)
            n1_ref[h, :, sl] = n1
            c1_ref[h, :, sl] = c1
            r2s.append(r2)
            e2s.append(e2)
            bads.append(b)
        for wt in range(tq // WTILE):
            r2_ref[h, wt] = jnp.concatenate(r2s[wt * per:(wt + 1) * per], axis=1).astype(BF16)
            e2_ref[h, wt] = jnp.concatenate(e2s[wt * per:(wt + 1) * per], axis=1).astype(BF16)
        return jnp.maximum(bad, jnp.concatenate(bads, axis=1))

    bad = lax.fori_loop(0, PEER_HEADS, functools.partial(per_head, False), jnp.zeros((1, tq), F32))

    @pl.when(jnp.max(bad) > 0.0)
    def _():
        lax.fori_loop(0, PEER_HEADS, functools.partial(per_head, True), jnp.zeros((1, tq), F32))


def _route(h2, wq3, keys, tq=256):
    t, d = h2.shape
    nh, _, qd = wq3.shape
    c3 = lambda i: (0, 0, 0)
    c4 = lambda i: (0, 0, 0, 0)
    tab = pl.BlockSpec((nh, PEER_KEYS, tq), lambda i: (0, 0, i))
    shp = jax.ShapeDtypeStruct((nh, PEER_KEYS, t), F32)
    tile = pl.BlockSpec((nh, tq // WTILE, PEER_KEYS, WTILE), lambda i: (0, i, 0, 0))
    tshp = jax.ShapeDtypeStruct((nh, t // WTILE, PEER_KEYS, WTILE), BF16)
    return pl.pallas_call(
        functools.partial(_route_kernel, tq=tq),
        grid=(t // tq,),
        in_specs=[pl.BlockSpec((tq, d), lambda i: (i, 0)),
                  pl.BlockSpec((nh, d, qd), c3),
                  pl.BlockSpec(keys.shape, c4)],
        out_specs=[tab, tab, tile, tile],
        out_shape=[shp, shp, tshp, tshp],
        compiler_params=_params("arbitrary"),
        name="route",
    )(h2, wq3, keys)


def _experts_kernel(h2_ref, u_ref, v_ref, n1_ref, c1_ref, r2_ref, e2_ref, x1_ref, g2_ref, o_ref, pt_ref,
                    w_ref, *, tb, eb):
    e = pl.program_id(1)
    nsub = eb // PEER_KEYS

    @pl.when(e == 0)
    def _():
        o_ref[...] = x1_ref[...]

    d = o_ref.shape[1]
    ntile = tb // LANES
    cw = d // nsub
    arows = 2 * PEER_KEYS

    def weights(j):
        pk = 2 * SUBLANES
        reps = PEER_KEYS // pk

        def rows(ref, h, sl):
            r = jnp.broadcast_to(ref[h, j:j + 1, sl], (pk, WTILE)).astype(BF16)
            return jnp.concatenate([r] * reps, axis=0)

        for wt in range(tb // WTILE):
            sl = slice(wt * WTILE, (wt + 1) * WTILE)
            w = jnp.zeros((PEER_KEYS, WTILE), BF16)
            for h in range(PEER_HEADS):
                e2 = e2_ref[h, wt]
                w = w + jnp.where(r2_ref[h, wt] < rows(n1_ref, h, sl), e2 * rows(c1_ref, h, sl),
                                  jnp.zeros_like(e2))
            w_ref[j, wt] = w

    def activate(c, act):
        for jj in range(arows // PEER_KEYS):
            j = c * (arows // PEER_KEYS) + jj
            for lt in range(ntile):
                sl = slice(lt * LANES, (lt + 1) * LANES)
                a = act[jj * PEER_KEYS:(jj + 1) * PEER_KEYS, sl]
                wl = (lt * LANES) % WTILE
                w = w_ref[j, lt * LANES // WTILE][:, wl:wl + LANES]
                p = _gelu_tanh(a) * w.astype(F32)
                pt_ref[sl, j * PEER_KEYS:(j + 1) * PEER_KEYS] = p.T.astype(BF16)

    @pl.when(e > 0)
    def _():
        for n in range(nsub):
            cs = slice(n * cw, (n + 1) * cw)
            o_ref[:, cs] += g2_ref[0][:, cs] * _dot(pt_ref[...], v_ref[:, cs])

    @pl.when(e < pl.num_programs(1) - 1)
    def _():
        prev_act = None
        for c in range(eb // arows):
            weights(2 * c)
            weights(2 * c + 1)
            act = _dot_nt(u_ref[c * arows:(c + 1) * arows, :], h2_ref[...])
            if prev_act is not None:
                activate(c - 1, prev_act)
            prev_act = act
        activate(eb // arows - 1, prev_act)


def _experts(h2, u, v, n1, c1, r2, e2, x1, g2, seq, tb=512, eb=1024):
    t, d = h2.shape
    ne = u.shape[0] // eb
    nh = n1.shape[0]
    nsub = eb // PEER_KEYS
    bs = seq // tb
    tok = lambda i, e: (i, 0)
    cur = lambda i, e: (jnp.minimum(e, ne - 1), 0)
    prev = lambda i, e: (jnp.maximum(e - 1, 0), 0)
    tab1 = pl.BlockSpec((nh, nsub, tb), lambda i, e: (0, jnp.minimum(e, ne - 1), i))
    tab2 = pl.BlockSpec((nh, tb // WTILE, PEER_KEYS, WTILE), lambda i, e: (0, i, 0, 0))
    return pl.pallas_call(
        functools.partial(_experts_kernel, tb=tb, eb=eb),
        grid=(t // tb, ne + 1),
        in_specs=[pl.BlockSpec((tb, d), tok), pl.BlockSpec((eb, d), cur), pl.BlockSpec((eb, d), prev),
                  tab1, tab1, tab2, tab2,
                  pl.BlockSpec((tb, d), tok), pl.BlockSpec((1, 1, d), lambda i, e: (i // bs, 0, 0))],
        out_specs=pl.BlockSpec((tb, d), tok),
        out_shape=jax.ShapeDtypeStruct((t, d), F32),
        scratch_shapes=[pltpu.VMEM((tb, eb), BF16), pltpu.VMEM((nsub, tb // WTILE, PEER_KEYS, WTILE), BF16)],
        compiler_params=_params("arbitrary", "arbitrary"),
        name="experts",
    )(h2, u, v, n1, c1, r2, e2, x1, g2)


def _rope_tables(positions):
    half = ROPE_DIM // 2
    inv_freq = jnp.power(ROPE_THETA, -2.0 * jnp.arange(half, dtype=F32) / ROPE_DIM)
    ang = positions.astype(F32).reshape(-1, 1) * inv_freq
    cos, sin = jnp.cos(ang), jnp.sin(ang)
    n = ang.shape[0]
    pad = HEAD_DIM - ROPE_DIM
    cos_h = jnp.concatenate([cos, cos, jnp.ones((n, pad), F32)], axis=1)
    sin_h = jnp.concatenate([-sin, sin, jnp.zeros((n, pad), F32)], axis=1)
    reps = LANES // HEAD_DIM
    return jnp.tile(cos_h, (1, reps)), jnp.tile(sin_h, (1, reps))


def _layer(x, c, positions, w_ada, b_ada, norm1_g, w_in, conv_w, conv_b, gate_a_w, gate_a_b,
           gate_x_w, gate_x_b, lam, q_norm_g, k_norm_g, sinks, out_norm_lru_g, out_norm_attn_g,
           w_out, norm2_g, peer_w_q, peer_sub_keys, peer_u, peer_v):
    b, s, d = x.shape
    t = b * s
    lw = conv_w.shape[1]
    qw = out_norm_attn_g.shape[0]
    x2 = x.reshape(t, d)
    row = lambda a: a.reshape(1, -1)

    c_pad = jnp.zeros((SUBLANES, d), F32).at[:b].set(c)
    ada = _ada(c_pad, w_ada, row(b_ada))[:b]
    sh1, sc1, g1, sh2, sc2, g2 = [a.reshape(b, 1, d) for a in jnp.split(ada, 6, axis=-1)]

    xl, yl, q, kv = _inproj(x2, row(norm1_g), sh1, sc1, w_in.astype(BF16), s, lw, qw)
    lru_n = _lru(xl, yl, conv_w, row(conv_b), gate_a_w.astype(BF16), row(gate_a_b),
                 gate_x_w.astype(BF16), row(gate_x_b), row(lam), row(out_norm_lru_g), b, s)
    cos_t, sin_t = _rope_tables(positions)
    reps = LANES // HEAD_DIM
    att_n = _attn(sinks, q, kv, cos_t, sin_t, row(jnp.tile(q_norm_g, reps)), row(jnp.tile(k_norm_g, reps)),
                  row(out_norm_attn_g), b, s)
    x1, h2 = _outproj(lru_n, att_n, x2, w_out.astype(BF16), g1, row(norm2_g), sh2, sc2, s)

    nh = peer_sub_keys.shape[0]
    wq3 = peer_w_q.astype(BF16).reshape(d, nh, -1).transpose(1, 0, 2)
    n1, c1, r2, e2 = _route(h2, wq3, peer_sub_keys.astype(BF16))
    out = _experts(h2, peer_u.astype(BF16), peer_v.astype(BF16), n1, c1, r2, e2, x1, g2, s)
    return out.reshape(b, s, d)


def kernel(x, c, positions, w_ada, b_ada, norm1_g, w_in, conv_w, conv_b, lru_gate_a_w, lru_gate_a_b,
           lru_gate_x_w, lru_gate_x_b, lru_lambda, q_norm_g, k_norm_g, attn_sinks, out_norm_lru_g,
           out_norm_attn_g, w_out, norm2_g, peer_w_q, peer_sub_keys, peer_u, peer_v):
    for l in range(w_ada.shape[0]):
        x = _layer(x, c, positions, w_ada[l], b_ada[l], norm1_g[l], w_in[l], conv_w[l], conv_b[l],
                   lru_gate_a_w[l], lru_gate_a_b[l], lru_gate_x_w[l], lru_gate_x_b[l], lru_lambda[l],
                   q_norm_g[l], k_norm_g[l], attn_sinks[l], out_norm_lru_g[l], out_norm_attn_g[l],
                   w_out[l], norm2_g[l], peer_w_q[l], peer_sub_keys[l], peer_u[l], peer_v[l])
    return x
```

```python
import functools
import math

import jax
import jax.numpy as jnp
from jax import lax
from jax.experimental import pallas as pl
from jax.experimental.pallas import tpu as pltpu

F32 = jnp.float32
BF16 = jnp.bfloat16

LRU_BLOCKS = 8
CONV_WIDTH = 4
LRU_C = 8.0
HEAD_DIM = 64
KV_HEADS = 2
WINDOW = 128
ROPE_THETA = 500000.0
ROPE_DIM = HEAD_DIM // 4
PEER_HEADS = 8
PEER_KEYS = 128
PEER_TOPK = 16
NORM_EPS = 1e-6

LANES = 128
SUBLANES = 8
VMEM_LIMIT = 56 * 1024 * 1024
WTILE = 2 * LANES

NEG_BIG = -1e30


def _dot(a, b):
    return jnp.dot(a, b, preferred_element_type=F32)


def _dot_nt(a, b):
    return lax.dot_general(a, b, (((1,), (1,)), ((), ())), preferred_element_type=F32)


def _split(x):
    hi = x.astype(BF16)
    lo = (x - hi.astype(F32)).astype(BF16)
    return hi, lo


def _dot3(a, b):
    ah, al = _split(a)
    bh, bl = _split(b)
    return _dot(ah, bh) + (_dot(ah, bl) + _dot(al, bh))


def _sigmoid(x):
    return 1.0 / (1.0 + jnp.exp(-x))


def _gelu_tanh(x):
    c = math.sqrt(2.0 / math.pi)
    return 0.5 * x * (1.0 + jnp.tanh(c * (x + 0.044715 * (x * x * x))))


def _params(*sem, flags=None):
    return pltpu.CompilerParams(dimension_semantics=sem, vmem_limit_bytes=VMEM_LIMIT, flags=flags)


def _ada_kernel(c_ref, w_ref, b_ref, o_ref):
    c = c_ref[...]
    s = c * _sigmoid(c)
    o_ref[...] = _dot3(s, w_ref[...]) + b_ref[...]


def _ada(c_pad, w_ada, b_ada, tn=1024):
    m, d = c_pad.shape
    n = w_ada.shape[1]
    return pl.pallas_call(
        _ada_kernel,
        grid=(n // tn,),
        in_specs=[pl.BlockSpec((m, d), lambda j: (0, 0)),
                  pl.BlockSpec((d, tn), lambda j: (0, j)),
                  pl.BlockSpec((1, tn), lambda j: (0, j))],
        out_specs=pl.BlockSpec((m, tn), lambda j: (0, j)),
        out_shape=jax.ShapeDtypeStruct((m, n), F32),
        compiler_params=_params("arbitrary"),
        name="ada",
    )(c_pad, w_ada, b_ada)


def _inproj_kernel(x_ref, g_ref, sh_ref, sc_ref, w_ref, xl_ref, yl_ref, q_ref, kv_ref, *, lw, qw):
    x = x_ref[...]
    ms = jnp.mean(x * x, axis=-1, keepdims=True)
    xn = x * lax.rsqrt(ms + NORM_EPS) * g_ref[...]
    h = (xn * (1.0 + sc_ref[0]) + sh_ref[0]).astype(BF16)
    xl_ref[...] = _dot(h, w_ref[:, 0:lw])
    yl_ref[...] = _dot(h, w_ref[:, lw:2 * lw])
    q_ref[...] = _dot(h, w_ref[:, 2 * lw:2 * lw + qw])
    kv_ref[...] = _dot(h, w_ref[:, 2 * lw + qw:])


def _inproj(x2, norm_g, sh, sc, w_in, seq, lw, qw, tm=256):
    t, d = x2.shape
    n = w_in.shape[1]
    kvw = n - 2 * lw - qw
    bs = seq // tm
    row = lambda i: (i, 0)
    vec = lambda i: (i // bs, 0, 0)
    return pl.pallas_call(
        functools.partial(_inproj_kernel, lw=lw, qw=qw),
        grid=(t // tm,),
        in_specs=[pl.BlockSpec((tm, d), row),
                  pl.BlockSpec((1, d), lambda i: (0, 0)),
                  pl.BlockSpec((1, 1, d), vec),
                  pl.BlockSpec((1, 1, d), vec),
                  pl.BlockSpec((d, n), lambda i: (0, 0))],
        out_specs=[pl.BlockSpec((tm, lw), row), pl.BlockSpec((tm, lw), row),
                   pl.BlockSpec((tm, qw), row), pl.BlockSpec((tm, kvw), row)],
        out_shape=[jax.ShapeDtypeStruct((t, lw), F32), jax.ShapeDtypeStruct((t, lw), F32),
                   jax.ShapeDtypeStruct((t, qw), F32), jax.ShapeDtypeStruct((t, kvw), F32)],
        compiler_params=_params("arbitrary"),
        name="inproj",
    )(x2, norm_g, sh, sc, w_in)


def _lru_kernel(xl_ref, yl_ref, cw_ref, cb_ref, wa_ref, ba_ref, wx_ref, bx_ref, lam_ref, g_ref,
                o_ref, xext, hcar, *, tc):
    lw = xl_ref.shape[1]
    blk = lw // LRU_BLOCKS

    @pl.when(pl.program_id(1) == 0)
    def _():
        xext[0:SUBLANES, :] = jnp.zeros((SUBLANES, lw), F32)
        hcar[...] = jnp.zeros_like(hcar)

    x = xl_ref[...]
    xext[SUBLANES:SUBLANES + tc, :] = x
    xc = cb_ref[...] + cw_ref[CONV_WIDTH - 1:CONV_WIDTH, :] * x
    for k in range(CONV_WIDTH - 1):
        off = SUBLANES - (CONV_WIDTH - 1) + k
        xc = xc + cw_ref[k:k + 1, :] * xext[off:off + tc, :]
    xext[0:SUBLANES, :] = x[tc - SUBLANES:tc, :]

    ra, rx = [], []
    for hb in range(LRU_BLOCKS):
        xb = xc[:, hb * blk:(hb + 1) * blk].astype(BF16)
        ra.append(_dot(xb, wa_ref[hb]))
        rx.append(_dot(xb, wx_ref[hb]))
    r = _sigmoid(jnp.concatenate(ra, axis=1) + ba_ref[...])
    gi = _sigmoid(jnp.concatenate(rx, axis=1) + bx_ref[...])

    nl = -lam_ref[...]
    softplus = jnp.maximum(nl, 0.0) + jnp.log1p(jnp.exp(-jnp.abs(nl)))
    log_a = (-LRU_C) * r * softplus
    a = jnp.exp(log_a)
    th = jnp.tanh(log_a)
    b = jnp.sqrt(-2.0 * th / (1.0 - th)) * (gi * xc)

    rows = lax.broadcasted_iota(jnp.int32, (tc, 1), 0)
    d = 1
    while d < tc:
        keep = rows >= d
        a_sh = jnp.where(keep, pltpu.roll(a, d, 0), 1.0)
        b_sh = jnp.where(keep, pltpu.roll(b, d, 0), 0.0)
        b = a * b_sh + b
        a = a * a_sh
        d *= 2
    h = b + a * hcar[0:1, :]
    hcar[...] = jnp.broadcast_to(h[tc - 1:tc, :], hcar.shape)

    y = h * _gelu_tanh(yl_ref[...])
    ms = jnp.mean(y * y, axis=-1, keepdims=True)
    o_ref[...] = (y * lax.rsqrt(ms + NORM_EPS) * g_ref[...]).astype(BF16)


def _lru(xl, yl, conv_w, conv_b, wa, ba, wx, bx, lam, g, batch, seq, tc=256):
    t, lw = xl.shape
    nb = seq // tc
    row = lambda b, j: (b * nb + j, 0)
    c2 = lambda b, j: (0, 0)
    c3 = lambda b, j: (0, 0, 0)
    blk = lw // LRU_BLOCKS
    return pl.pallas_call(
        functools.partial(_lru_kernel, tc=tc),
        grid=(batch, nb),
        in_specs=[pl.BlockSpec((tc, lw), row), pl.BlockSpec((tc, lw), row),
                  pl.BlockSpec((CONV_WIDTH, lw), c2), pl.BlockSpec((1, lw), c2),
                  pl.BlockSpec((LRU_BLOCKS, blk, blk), c3), pl.BlockSpec((1, lw), c2),
                  pl.BlockSpec((LRU_BLOCKS, blk, blk), c3), pl.BlockSpec((1, lw), c2),
                  pl.BlockSpec((1, lw), c2), pl.BlockSpec((1, lw), c2)],
        out_specs=pl.BlockSpec((tc, lw), row),
        out_shape=jax.ShapeDtypeStruct((t, lw), BF16),
        scratch_shapes=[pltpu.VMEM((tc + SUBLANES, lw), F32), pltpu.VMEM((SUBLANES, lw), F32)],
        compiler_params=_params("arbitrary", "arbitrary"),
        name="lru",
    )(xl, yl, conv_w, conv_b, wa, ba, wx, bx, lam, g)


def _head_norm_rope(x, g, cos, sin, bd):
    xx = x * x
    hi, lo = _split(xx)
    ss = _dot(hi, bd) + _dot(lo, bd)
    xn = x * lax.rsqrt(ss * (1.0 / HEAD_DIM) + NORM_EPS) * g
    lane = lax.broadcasted_iota(jnp.int32, (1, LANES), 1) % HEAD_DIM
    half = ROPE_DIM // 2
    sw = jnp.where(lane < half, pltpu.roll(xn, LANES - half, 1), pltpu.roll(xn, half, 1))
    return xn * cos + sw * sin


def _attn_kernel(sink_ref, q_ref, kvc_ref, kvp_ref, cosc_ref, sinc_ref, cosp_ref, sinp_ref,
                 qg_ref, kg_ref, og_ref, o_ref, *, qb):
    first = pl.program_id(1) == 0
    nsub = qb // WINDOW
    qw = q_ref.shape[1]
    npair = qw // LANES
    pairs_per_group = npair // KV_HEADS

    li = lax.broadcasted_iota(jnp.int32, (LANES, LANES), 0) // HEAD_DIM
    lj = lax.broadcasted_iota(jnp.int32, (LANES, LANES), 1) // HEAD_DIM
    bd = (li == lj).astype(BF16)
    lane = lax.broadcasted_iota(jnp.int32, (1, LANES), 1)
    lo_half = lane < HEAD_DIM

    cosc, sinc = cosc_ref[...], sinc_ref[...]
    kc = _head_norm_rope(kvc_ref[:, 0:LANES], kg_ref[...], cosc, sinc, bd)
    kp = _head_norm_rope(kvp_ref[:, 0:LANES], kg_ref[...], cosp_ref[...], sinp_ref[...], bd)
    kfull = jnp.concatenate([kp, kc], axis=0)
    vfull = jnp.concatenate([kvp_ref[:, LANES:2 * LANES], kvc_ref[:, LANES:2 * LANES]], axis=0)

    def variants(t):
        a0 = jnp.where(lo_half, t, 0.0)
        b1 = jnp.where(lo_half, 0.0, t)
        b0 = pltpu.roll(a0, HEAD_DIM, 1)
        a1 = pltpu.roll(b1, HEAD_DIM, 1)
        return ((a0.astype(BF16), b0.astype(BF16)), (a1.astype(BF16), b1.astype(BF16)))

    kvar = variants(kfull)
    vvar = variants(vfull)

    qi = lax.broadcasted_iota(jnp.int32, (WINDOW, 2 * WINDOW), 0)
    kj = lax.broadcasted_iota(jnp.int32, (WINDOW, 2 * WINDOW), 1)
    band = (kj > qi) & (kj <= qi + WINDOW)
    bias_inner = jnp.where(band, 0.0, NEG_BIG)
    bias_first = jnp.where(band & (kj >= WINDOW), 0.0, NEG_BIG)
    bias0 = jnp.where(first, bias_first, bias_inner)

    scale = HEAD_DIM ** -0.5
    for s in range(nsub):
        r0 = s * WINDOW
        bias = bias0 if s == 0 else bias_inner
        outs = []
        for j in range(npair):
            g = j // pairs_per_group
            qp = _head_norm_rope(q_ref[r0:r0 + WINDOW, j * LANES:(j + 1) * LANES], qg_ref[...],
                                 cosc[r0:r0 + WINDOW], sinc[r0:r0 + WINDOW], bd)
            qp = (qp * scale).astype(BF16)
            acc = None
            for e in range(2):
                kk = kvar[g][e][r0:r0 + 2 * WINDOW]
                vv = vvar[g][e][r0:r0 + 2 * WINDOW]
                sc = _dot_nt(qp, kk) + bias
                sink = sink_ref[2 * j + e]
                m = jnp.maximum(jnp.max(sc, axis=-1, keepdims=True), sink)
                p = jnp.exp(sc - m)
                denom = jnp.sum(p, axis=-1, keepdims=True) + jnp.exp(sink - m)
                o = _dot(p.astype(BF16), vv) * (1.0 / denom)
                acc = o if acc is None else acc + o
            outs.append(acc)
        y = jnp.concatenate(outs, axis=1)
        ms = jnp.mean(y * y, axis=-1, keepdims=True)
        o_ref[r0:r0 + WINDOW, :] = (y * lax.rsqrt(ms + NORM_EPS) * og_ref[...]).astype(BF16)


def _attn(sinks, q, kv, cos_t, sin_t, qg, kg, og, batch, seq, qb=512):
    t, qw = q.shape
    kvw = kv.shape[1]
    nb = seq // qb
    sub = qb // WINDOW
    cur = lambda b, i: (b * nb + i, 0)
    prev = lambda b, i: (jnp.maximum((b * nb + i) * sub - 1, 0), 0)
    c2 = lambda b, i: (0, 0)
    return pl.pallas_call(
        functools.partial(_attn_kernel, qb=qb),
        grid=(batch, nb),
        in_specs=[pl.BlockSpec(memory_space=pltpu.SMEM),
                  pl.BlockSpec((qb, qw), cur),
                  pl.BlockSpec((qb, kvw), cur), pl.BlockSpec((WINDOW, kvw), prev),
                  pl.BlockSpec((qb, LANES), cur), pl.BlockSpec((qb, LANES), cur),
                  pl.BlockSpec((WINDOW, LANES), prev), pl.BlockSpec((WINDOW, LANES), prev),
                  pl.BlockSpec((1, LANES), c2), pl.BlockSpec((1, LANES), c2),
                  pl.BlockSpec((1, qw), c2)],
        out_specs=pl.BlockSpec((qb, qw), cur),
        out_shape=jax.ShapeDtypeStruct((t, qw), BF16),
        compiler_params=_params("arbitrary", "arbitrary"),
        name="attn",
    )(sinks, q, kv, kv, cos_t, sin_t, cos_t, sin_t, qg, kg, og)


def _outproj_kernel(lru_ref, att_ref, x_ref, w_ref, g1_ref, ng_ref, sh_ref, sc_ref, x1_ref, h2_ref):
    lw = lru_ref.shape[1]
    mix = _dot(lru_ref[...], w_ref[0:lw, :]) + _dot(att_ref[...], w_ref[lw:, :])
    x1 = x_ref[...] + g1_ref[0] * mix
    x1_ref[...] = x1
    ms = jnp.mean(x1 * x1, axis=-1, keepdims=True)
    xn = x1 * lax.rsqrt(ms + NORM_EPS) * ng_ref[...]
    h2_ref[...] = (xn * (1.0 + sc_ref[0]) + sh_ref[0]).astype(BF16)


def _outproj(lru_n, att_n, x2, w_out, g1, norm_g, sh, sc, seq, tm=256):
    t, d = x2.shape
    lw = lru_n.shape[1]
    aw = att_n.shape[1]
    bs = seq // tm
    row = lambda i: (i, 0)
    vec = lambda i: (i // bs, 0, 0)
    c2 = lambda i: (0, 0)
    return pl.pallas_call(
        _outproj_kernel,
        grid=(t // tm,),
        in_specs=[pl.BlockSpec((tm, lw), row), pl.BlockSpec((tm, aw), row), pl.BlockSpec((tm, d), row),
                  pl.BlockSpec((lw + aw, d), c2), pl.BlockSpec((1, 1, d), vec), pl.BlockSpec((1, d), c2),
                  pl.BlockSpec((1, 1, d), vec), pl.BlockSpec((1, 1, d), vec)],
        out_specs=[pl.BlockSpec((tm, d), row), pl.BlockSpec((tm, d), row)],
        out_shape=[jax.ShapeDtypeStruct((t, d), F32), jax.ShapeDtypeStruct((t, d), BF16)],
        compiler_params=_params("arbitrary"),
        name="outproj",
    )(lru_n, att_n, x2, w_out, g1, norm_g, sh, sc)


def _top16(s, exact):
    r = s.shape[0]
    rowid = lax.broadcasted_iota(jnp.int32, s.shape, 0).astype(F32) if exact else None
    rank = jnp.full(s.shape, float(PEER_TOPK), F32)
    vals = []
    for k in range(PEER_TOPK):
        m = jnp.max(s, axis=0, keepdims=True)
        sel = s == m
        if exact:
            first = jnp.min(jnp.where(sel, rowid, float(r)), axis=0, keepdims=True)
            sel = rowid == first
        rank = jnp.where(sel, float(k), rank)
        s = jnp.where(sel, -jnp.inf, s)
        vals.append(m)
    taken = jnp.sum(jnp.where(rank < float(PEER_TOPK), 1.0, 0.0), axis=0, keepdims=True)
    return rank, vals, jnp.where(taken == float(PEER_TOPK), 0.0, 1.0)


_CAND_LIMS = [PEER_TOPK // (k + 1) for k in range(PEER_TOPK)]


def _route_tile(s1, s2, exact):
    rank1, v1, bad1 = _top16(s1, exact)
    rank2, v2, bad2 = _top16(s2, exact)
    v2s = jnp.concatenate(v2, axis=0)
    v1s = jnp.concatenate(v1, axis=0)
    sub = lax.broadcasted_iota(jnp.int32, (SUBLANES, 1), 0)
    pieces = [v1[0] + v2s]
    for k1 in range(1, SUBLANES):
        pieces.append(jnp.where(sub < _CAND_LIMS[k1], v1[k1] + v2s[0:SUBLANES], -jnp.inf))
    pieces.append(v1s[SUBLANES:] + v2[0])
    cand = jnp.concatenate(pieces, axis=0)
    nrow = cand.shape[0]
    rowid = lax.broadcasted_iota(jnp.int32, cand.shape, 0).astype(F32) if exact else None
    cmax = v1[0] + v2[0]
    ex = jnp.exp(cand - cmax)
    c = cand
    for _ in range(PEER_TOPK):
        m = jnp.max(c, axis=0, keepdims=True)
        sel = c == m
        if exact:
            first = jnp.min(jnp.where(sel, rowid, float(nrow)), axis=0, keepdims=True)
            sel = rowid == first
        c = jnp.where(sel, -jnp.inf, c)
    selected = jnp.where(c != cand, 1.0, 0.0)
    taken = jnp.sum(selected, axis=0, keepdims=True)
    bad = jnp.maximum(jnp.maximum(bad1, bad2), jnp.where(taken == float(PEER_TOPK), 0.0, 1.0))
    z = jnp.sum(selected * ex, axis=0, keepdims=True)
    counts = [jnp.sum(selected[0:2 * SUBLANES], axis=0, keepdims=True)]
    for k1 in range(1, SUBLANES):
        off = (k1 + 1) * SUBLANES
        counts.append(jnp.sum(selected[off:off + SUBLANES], axis=0, keepdims=True))
    tail = selected[nrow - SUBLANES:nrow]
    for k1 in range(SUBLANES, PEER_TOPK):
        counts.append(tail[k1 - SUBLANES:k1 - SUBLANES + 1])
    n1 = jnp.zeros(s1.shape, F32)
    for k1 in range(PEER_TOPK):
        n1 = jnp.where(rank1 == float(k1), counts[k1], n1)
    c1 = jnp.exp(s1 - v1[0]) * (1.0 / z)
    e2 = jnp.exp(s2 - v2[0])
    return n1, c1, rank2, e2, bad


def _route_kernel(h2_ref, wq_ref, keys_ref, n1_ref, c1_ref, r2_ref, e2_ref, bad_ref, flag_ref, *, tq):
    half = PEER_KEYS
    per = WTILE // LANES

    def per_head(exact, h, carry):
        qh = _dot(h2_ref[...], wq_ref[h])
        s1 = _dot_nt(keys_ref[h, 0], qh[:, 0:half].astype(BF16))
        s2 = _dot_nt(keys_ref[h, 1], qh[:, half:2 * half].astype(BF16))
        r2s, e2s, bads = [], [], []
        for lt in range(tq // LANES):
            sl = slice(lt * LANES, (lt + 1) * LANES)
            n1, c1, r2, e2, b = _route_tile(s1[:, sl], s2[:, sl], exact)
            n1_ref[h, :, sl] = n1
            c1_ref[h, :, sl] = c1
            r2s.append(r2)
            e2s.append(e2)
            bads.append(b)
        for wt in range(tq // WTILE):
            r2_ref[h, wt] = jnp.concatenate(r2s[wt * per:(wt + 1) * per], axis=1).astype(BF16)
            e2_ref[h, wt] = jnp.concatenate(e2s[wt * per:(wt + 1) * per], axis=1).astype(BF16)
        if not exact:
            bad_ref[h] = jnp.broadcast_to(jnp.concatenate(bads, axis=1), bad_ref.shape[1:])
        return carry

    lax.fori_loop(0, PEER_HEADS, functools.partial(per_head, False), 0)
    for h in range(PEER_HEADS):
        flag_ref[h] = jnp.max(bad_ref[h])

    def redo(h, carry):
        @pl.when(flag_ref[h] > 0.0)
        def _():
            per_head(True, h, 0)
        return carry

    lax.fori_loop(0, PEER_HEADS, redo, 0)


def _route(h2, wq3, keys, tq=256):
    t, d = h2.shape
    nh, _, qd = wq3.shape
    c3 = lambda i: (0, 0, 0)
    c4 = lambda i: (0, 0, 0, 0)
    tab = pl.BlockSpec((nh, PEER_KEYS, tq), lambda i: (0, 0, i))
    shp = jax.ShapeDtypeStruct((nh, PEER_KEYS, t), F32)
    tile = pl.BlockSpec((nh, tq // WTILE, PEER_KEYS, WTILE), lambda i: (0, i, 0, 0))
    tshp = jax.ShapeDtypeStruct((nh, t // WTILE, PEER_KEYS, WTILE), BF16)
    return pl.pallas_call(
        functools.partial(_route_kernel, tq=tq),
        grid=(t // tq,),
        in_specs=[pl.BlockSpec((tq, d), lambda i: (i, 0)),
                  pl.BlockSpec((nh, d, qd), c3),
                  pl.BlockSpec(keys.shape, c4)],
        out_specs=[tab, tab, tile, tile],
        out_shape=[shp, shp, tshp, tshp],
        scratch_shapes=[pltpu.VMEM((nh, SUBLANES, tq), F32), pltpu.SMEM((nh,), F32)],
        compiler_params=_params("arbitrary"),
        name="route",
    )(h2, wq3, keys)


def _experts_kernel(h2_ref, u_ref, v_ref, n1_ref, c1_ref, r2_ref, e2_ref, x1_ref, g2_ref, o_ref, pt_ref,
                    w_ref, *, tb, eb):
    e = pl.program_id(1)
    nsub = eb // PEER_KEYS

    @pl.when(e == 0)
    def _():
        o_ref[...] = x1_ref[...]

    d = o_ref.shape[1]
    ntile = tb // LANES
    cw = d // nsub
    arows = 2 * PEER_KEYS

    def weights(j):
        pk = 2 * SUBLANES
        reps = PEER_KEYS // pk

        def rows(ref, h, sl):
            r = jnp.broadcast_to(ref[h, j:j + 1, sl], (pk, WTILE)).astype(BF16)
            return jnp.concatenate([r] * reps, axis=0)

        for wt in range(tb // WTILE):
            sl = slice(wt * WTILE, (wt + 1) * WTILE)
            w = jnp.zeros((PEER_KEYS, WTILE), BF16)
            for h in range(PEER_HEADS):
                e2 = e2_ref[h, wt]
                w = w + jnp.where(r2_ref[h, wt] < rows(n1_ref, h, sl), e2 * rows(c1_ref, h, sl),
                                  jnp.zeros_like(e2))
            w_ref[j, wt] = w

    def activate(c, act):
        for jj in range(arows // PEER_KEYS):
            j = c * (arows // PEER_KEYS) + jj
            for lt in range(ntile):
                sl = slice(lt * LANES, (lt + 1) * LANES)
                a = act[jj * PEER_KEYS:(jj + 1) * PEER_KEYS, sl]
                wl = (lt * LANES) % WTILE
                w = w_ref[j, lt * LANES // WTILE][:, wl:wl + LANES]
                p = _gelu_tanh(a) * w.astype(F32)
                pt_ref[sl, j * PEER_KEYS:(j + 1) * PEER_KEYS] = p.T.astype(BF16)

    @pl.when(e > 0)
    def _():
        for n in range(nsub):
            cs = slice(n * cw, (n + 1) * cw)
            o_ref[:, cs] += g2_ref[0][:, cs] * _dot(pt_ref[...], v_ref[:, cs])

    @pl.when(e < pl.num_programs(1) - 1)
    def _():
        prev_act = None
        for c in range(eb // arows):
            weights(2 * c)
            weights(2 * c + 1)
            act = _dot_nt(u_ref[c * arows:(c + 1) * arows, :], h2_ref[...])
            if prev_act is not None:
                activate(c - 1, prev_act)
            prev_act = act
        activate(eb // arows - 1, prev_act)


def _experts(h2, u, v, n1, c1, r2, e2, x1, g2, seq, tb=512, eb=1024):
    t, d = h2.shape
    ne = u.shape[0] // eb
    nh = n1.shape[0]
    nsub = eb // PEER_KEYS
    bs = seq // tb
    tok = lambda i, e: (i, 0)
    cur = lambda i, e: (jnp.minimum(e, ne - 1), 0)
    prev = lambda i, e: (jnp.maximum(e - 1, 0), 0)
    tab1 = pl.BlockSpec((nh, nsub, tb), lambda i, e: (0, jnp.minimum(e, ne - 1), i))
    tab2 = pl.BlockSpec((nh, tb // WTILE, PEER_KEYS, WTILE), lambda i, e: (0, i, 0, 0))
    return pl.pallas_call(
        functools.partial(_experts_kernel, tb=tb, eb=eb),
        grid=(t // tb, ne + 1),
        in_specs=[pl.BlockSpec((tb, d), tok), pl.BlockSpec((eb, d), cur), pl.BlockSpec((eb, d), prev),
                  tab1, tab1, tab2, tab2,
                  pl.BlockSpec((tb, d), tok), pl.BlockSpec((1, 1, d), lambda i, e: (i // bs, 0, 0))],
        out_specs=pl.BlockSpec((tb, d), tok),
        out_shape=jax.ShapeDtypeStruct((t, d), F32),
        scratch_shapes=[pltpu.VMEM((tb, eb), BF16), pltpu.VMEM((nsub, tb // WTILE, PEER_KEYS, WTILE), BF16)],
        compiler_params=_params("arbitrary", "arbitrary"),
        name="experts",
    )(h2, u, v, n1, c1, r2, e2, x1, g2)


def _rope_tables(positions):
    half = ROPE_DIM // 2
    inv_freq = jnp.power(ROPE_THETA, -2.0 * jnp.arange(half, dtype=F32) / ROPE_DIM)
    ang = positions.astype(F32).reshape(-1, 1) * inv_freq
    cos, sin = jnp.cos(ang), jnp.sin(ang)
    n = ang.shape[0]
    pad = HEAD_DIM - ROPE_DIM
    cos_h = jnp.concatenate([cos, cos, jnp.ones((n, pad), F32)], axis=1)
    sin_h = jnp.concatenate([-sin, sin, jnp.zeros((n, pad), F32)], axis=1)
    reps = LANES // HEAD_DIM
    return jnp.tile(cos_h, (1, reps)), jnp.tile(sin_h, (1, reps))


def _layer(x, c, positions, w_ada, b_ada, norm1_g, w_in, conv_w, conv_b, gate_a_w, gate_a_b,
           gate_x_w, gate_x_b, lam, q_norm_g, k_norm_g, sinks, out_norm_lru_g, out_norm_attn_g,
           w_out, norm2_g, peer_w_q, peer_sub_keys, peer_u, peer_v):
    b, s, d = x.shape
    t = b * s
    lw = conv_w.shape[1]
    qw = out_norm_attn_g.shape[0]
    x2 = x.reshape(t, d)
    row = lambda a: a.reshape(1, -1)

    c_pad = jnp.zeros((SUBLANES, d), F32).at[:b].set(c)
    ada = _ada(c_pad, w_ada, row(b_ada))[:b]
    sh1, sc1, g1, sh2, sc2, g2 = [a.reshape(b, 1, d) for a in jnp.split(ada, 6, axis=-1)]

    xl, yl, q, kv = _inproj(x2, row(norm1_g), sh1, sc1, w_in.astype(BF16), s, lw, qw)
    lru_n = _lru(xl, yl, conv_w, row(conv_b), gate_a_w.astype(BF16), row(gate_a_b),
                 gate_x_w.astype(BF16), row(gate_x_b), row(lam), row(out_norm_lru_g), b, s)
    cos_t, sin_t = _rope_tables(positions)
    reps = LANES // HEAD_DIM
    att_n = _attn(sinks, q, kv, cos_t, sin_t, row(jnp.tile(q_norm_g, reps)), row(jnp.tile(k_norm_g, reps)),
                  row(out_norm_attn_g), b, s)
    x1, h2 = _outproj(lru_n, att_n, x2, w_out.astype(BF16), g1, row(norm2_g), sh2, sc2, s)

    nh = peer_sub_keys.shape[0]
    wq3 = peer_w_q.astype(BF16).reshape(d, nh, -1).transpose(1, 0, 2)
    n1, c1, r2, e2 = _route(h2, wq3, peer_sub_keys.astype(BF16))
    out = _experts(h2, peer_u.astype(BF16), peer_v.astype(BF16), n1, c1, r2, e2, x1, g2, s)
    return out.reshape(b, s, d)


def kernel(x, c, positions, w_ada, b_ada, norm1_g, w_in, conv_w, conv_b, lru_gate_a_w, lru_gate_a_b,
           lru_gate_x_w, lru_gate_x_b, lru_lambda, q_norm_g, k_norm_g, attn_sinks, out_norm_lru_g,
           out_norm_attn_g, w_out, norm2_g, peer_w_q, peer_sub_keys, peer_u, peer_v):
    for l in range(w_ada.shape[0]):
        x = _layer(x, c, positions, w_ada[l], b_ada[l], norm1_g[l], w_in[l], conv_w[l], conv_b[l],
                   lru_gate_a_w[l], lru_gate_a_b[l], lru_gate_x_w[l], lru_gate_x_b[l], lru_lambda[l],
                   q_norm_g[l], k_norm_g[l], attn_sinks[l], out_norm_lru_g[l], out_norm_attn_g[l],
                   w_out[l], norm2_g[l], peer_w_q[l], peer_sub_keys[l], peer_u[l], peer_v[l])
    return x
```
